```python
import math
import jax, jax.numpy as jnp
from jax import lax
import numpy as np

D_MODEL = 1024
BATCH = 8
SEQ = 4096
DEPTH = 2

GRID_W = 64
CTX_LEN = 256
N_EVEN = (DEPTH + 1) // 2
N_ODD = DEPTH // 2
LAST_EVEN = (DEPTH - 1) - (DEPTH - 1) % 2

EPS = 1e-6
N_MOD = 9
D_FF = 2816

A_HEADS = D_MODEL // 256
A_HEAD_DIM = 64
A_V_DIM = 2 * A_HEAD_DIM
A_WIDTH = A_HEADS * A_V_DIM
Q_BLOCK = 128
ROPE_BASE = 10000.0
ROPE_FREQS = A_HEAD_DIM // 4

B_GROUPS = D_MODEL // 256
B_GROUP_CH = 128
B_WIDTH = B_GROUPS * B_GROUP_CH
B_CHUNK = 128

Q_COLS = A_HEADS * 2 * A_HEAD_DIM
K_COLS = A_HEADS * 2 * A_HEAD_DIM
V_COLS = A_HEADS * A_V_DIM
Z_COLS = 2 * B_WIDTH
IN_COLS = Q_COLS + K_COLS + V_COLS + Z_COLS
MIX_WIDTH = A_WIDTH + B_WIDTH

C_GROUPS = 4
C_GROUP_CH = D_MODEL // C_GROUPS

kernel_name = "hybrid_diffattn_gmlp_fnet_macaron_dit"


def rmsnorm(x, g=None):
    xf = x.astype(jnp.float32)
    y = xf * lax.rsqrt(jnp.mean(xf * xf, axis=-1, keepdims=True) + EPS)
    if g is not None:
        y = y * g.astype(jnp.float32)
    return y.astype(x.dtype)


def layernorm(x, g, b):
    xf = x.astype(jnp.float32)
    mu = jnp.mean(xf, axis=-1, keepdims=True)
    var = jnp.mean(jnp.square(xf - mu), axis=-1, keepdims=True)
    y = (xf - mu) * lax.rsqrt(var + EPS) * g.astype(jnp.float32) + b.astype(jnp.float32)
    return y.astype(x.dtype)


def modulate(x, shift, scale):
    return rmsnorm(x) * (1 + scale) + shift


def swiglu(h, w_gu, w_down):
    g, u = jnp.split(h @ w_gu, 2, axis=-1)
    return (jax.nn.silu(g) * u) @ w_down


def axial_angles(L):
    rows = L // GRID_W
    row = jnp.repeat(jnp.arange(rows), GRID_W)
    col = jnp.tile(jnp.arange(GRID_W), rows)
    inv = ROPE_BASE ** (-jnp.arange(ROPE_FREQS, dtype=jnp.float32) / ROPE_FREQS)
    ang = jnp.stack([row[:, None] * inv, col[:, None] * inv], axis=1)
    return jnp.cos(ang), jnp.sin(ang)


def apply_axial_rope(x, cos, sin):
    shp = x.shape
    xr = x.reshape(shp[:-1] + (2, 2, ROPE_FREQS))
    xa, xb = xr[..., 0, :], xr[..., 1, :]
    c = cos[None, :, None, None].astype(x.dtype)
    s = sin[None, :, None, None].astype(x.dtype)
    out = jnp.stack([xa * c - xb * s, xb * c + xa * s], axis=-2)
    return out.reshape(shp)


def diff_attend(q, k, v, lam):
    s = jnp.einsum('bqhmd,bkhmd->bhmqk', q, k, preferred_element_type=jnp.float32) * (A_HEAD_DIM ** -0.5)
    p = jax.nn.softmax(s, axis=-1)
    a = p[:, :, 0] - lam * p[:, :, 1]
    return jnp.einsum('bhqk,bkhe->bqhe', a.astype(v.dtype), v)


def split_qkvz(proj):
    lead = proj.shape[:-1]
    q = proj[..., :Q_COLS].reshape(lead + (A_HEADS, 2, A_HEAD_DIM))
    k = proj[..., Q_COLS:Q_COLS + K_COLS].reshape(lead + (A_HEADS, 2, A_HEAD_DIM))
    v = proj[..., Q_COLS + K_COLS:Q_COLS + K_COLS + V_COLS].reshape(lead + (A_HEADS, A_V_DIM))
    z = proj[..., Q_COLS + K_COLS + V_COLS:]
    return q, k, v, z


def diff_head_out(o, subln_g, lam_init):
    y = rmsnorm(o, subln_g) * (1 - lam_init)
    return y.reshape(o.shape[:2] + (A_WIDTH,))


def spatial_gating(z, norm_g, norm_b, w_s, b_s):
    Bn, L, _ = z.shape
    u, vg = jnp.split(jax.nn.gelu(z), 2, axis=-1)
    vg = layernorm(vg.reshape(Bn, L, B_GROUPS, B_GROUP_CH), norm_g, norm_b)
    vg = vg.reshape(Bn, L // B_CHUNK, B_CHUNK, B_GROUPS, B_GROUP_CH)
    mixed = jnp.einsum('gpq,bnqgc->bnpgc', w_s, vg) + b_s.T[:, :, None]
    return u * mixed.reshape(Bn, L, B_WIDTH)


def even_mixer(h, hc, w_in, w_out, lam_vec, subln_g, sgu_g, sgu_bn, sgu_w, sgu_b, lam_init, cos, sin, ctx_full):
    Bn, L, _ = h.shape
    lv = lam_vec.astype(jnp.float32)
    lam = jnp.exp(jnp.sum(lv[0] * lv[1])) - jnp.exp(jnp.sum(lv[2] * lv[3])) + lam_init
    q, k, v, z = split_qkvz(h @ w_in)
    q = apply_axial_rope(q, cos, sin)
    k = apply_axial_rope(k, cos, sin)
    if ctx_full:
        qc, kc, vc, zc = split_qkvz(hc @ w_in)
    else:
        kv = hc @ w_in[:, Q_COLS:Q_COLS + K_COLS + V_COLS]
        kc = kv[..., :K_COLS].reshape(kv.shape[:2] + (A_HEADS, 2, A_HEAD_DIM))
        vc = kv[..., K_COLS:].reshape(kv.shape[:2] + (A_HEADS, A_V_DIM))
    k_all = jnp.concatenate([k, kc], axis=1)
    v_all = jnp.concatenate([v, vc], axis=1)
    qb = q.reshape(Bn, L // Q_BLOCK, Q_BLOCK, A_HEADS, 2, A_HEAD_DIM).swapaxes(0, 1)
    o = lax.map(lambda qi: diff_attend(qi, k_all, v_all, lam), qb)
    o = o.swapaxes(0, 1).reshape(Bn, L, A_HEADS, A_V_DIM)
    y = jnp.concatenate([diff_head_out(o, subln_g, lam_init),
                         spatial_gating(z, sgu_g, sgu_bn, sgu_w, sgu_b)], axis=-1) @ w_out
    yc = None
    if ctx_full:
        oc = diff_attend(qc, kc, vc, lam)
        yc = jnp.concatenate([diff_head_out(oc, subln_g, lam_init),
                              spatial_gating(zc, sgu_g, sgu_bn, sgu_w, sgu_b)], axis=-1) @ w_out
    return y, yc


def fourier_mixer(h, w_out, b_out):
    Bn, L, _ = h.shape
    hg = h.astype(jnp.float32).reshape(Bn, L, C_GROUPS, C_GROUP_CH)
    f = jnp.fft.fftn(hg, axes=(1, 3), norm='ortho').real
    return f.reshape(Bn, L, D_MODEL).astype(h.dtype) @ w_out + b_out


def setup_inputs(seed: int = 0) -> dict:
    key = jax.random.key(seed)
    ks = jax.random.split(key, 20)
    nrm = lambda k, shp, s: jax.random.normal(k, shp, jnp.float32) * s
    return {
        "x": nrm(ks[0], (BATCH, SEQ, D_MODEL), 1.0),
        "c": nrm(ks[1], (BATCH, D_MODEL), 1.0),
        "ctx": nrm(ks[2], (BATCH, CTX_LEN, D_MODEL), 1.0),
        "c_ctx": nrm(ks[3], (D_MODEL,), 1.0),
        "ada_w": nrm(ks[4], (DEPTH, D_MODEL, N_MOD * D_MODEL), D_MODEL ** -0.5),
        "ada_b": nrm(ks[5], (DEPTH, N_MOD * D_MODEL), 0.02),
        "ffn_w_gu": nrm(ks[6], (DEPTH, 2, D_MODEL, 2 * D_FF), D_MODEL ** -0.5),
        "ffn_w_down": nrm(ks[7], (DEPTH, 2, D_FF, D_MODEL), D_FF ** -0.5),
        "mix_w_in": nrm(ks[8], (N_EVEN, D_MODEL, IN_COLS), D_MODEL ** -0.5),
        "mix_w_out": nrm(ks[9], (N_EVEN, MIX_WIDTH, D_MODEL), MIX_WIDTH ** -0.5),
        "diff_lambda": nrm(ks[10], (N_EVEN, 4, A_HEAD_DIM), 0.1),
        "diff_subln_g": 1.0 + nrm(ks[11], (N_EVEN, A_V_DIM), 0.02),
        "sgu_norm_g": 1.0 + nrm(ks[12], (N_EVEN, B_GROUPS, B_GROUP_CH), 0.02),
        "sgu_norm_b": nrm(ks[13], (N_EVEN, B_GROUPS, B_GROUP_CH), 0.02),
        "sgu_w": nrm(ks[14], (N_EVEN, B_GROUPS, B_CHUNK, B_CHUNK), B_CHUNK ** -0.5),
        "sgu_b": 1.0 + nrm(ks[15], (N_EVEN, B_GROUPS, B_CHUNK), 0.02),
        "fourier_w_out": nrm(ks[16], (N_ODD, D_MODEL, D_MODEL), D_MODEL ** -0.5),
        "fourier_b_out": nrm(ks[17], (N_ODD, D_MODEL), 0.02),
        "final_norm_g": 1.0 + nrm(ks[18], (D_MODEL,), 0.02),
    }


def reference(x, c, ctx, c_ctx, ada_w, ada_b, ffn_w_gu, ffn_w_down, mix_w_in, mix_w_out, diff_lambda,
              diff_subln_g, sgu_norm_g, sgu_norm_b, sgu_w, sgu_b, fourier_w_out, fourier_b_out, final_norm_g):
    cos, sin = axial_angles(x.shape[1])
    cx = ctx
    for l in range(DEPTH):
        is_even = l % 2 == 0
        ctx_reads = l <= LAST_EVEN
        ctx_full = l < LAST_EVEN
        m = jnp.split((jax.nn.silu(c) @ ada_w[l] + ada_b[l])[:, None, :], N_MOD, axis=-1)
        x = x + 0.5 * m[2] * swiglu(modulate(x, m[0], m[1]), ffn_w_gu[l, 0], ffn_w_down[l, 0])
        if ctx_reads:
            mc = jnp.split(jax.nn.silu(c_ctx) @ ada_w[l] + ada_b[l], N_MOD, axis=-1)
            cx = cx + 0.5 * mc[2] * swiglu(modulate(cx, mc[0], mc[1]), ffn_w_gu[l, 0], ffn_w_down[l, 0])
        h = modulate(x, m[3], m[4])
        if is_even:
            e = l // 2
            lam_init = 0.8 - 0.6 * math.exp(-0.3 * l)
            hc = modulate(cx, mc[3], mc[4])
            y, yc = even_mixer(h, hc, mix_w_in[e], mix_w_out[e], diff_lambda[e], diff_subln_g[e],
                               sgu_norm_g[e], sgu_norm_b[e], sgu_w[e], sgu_b[e], lam_init, cos, sin, ctx_full)
            if ctx_full:
                cx = cx + mc[5] * yc
        else:
            o = l // 2
            y = fourier_mixer(h, fourier_w_out[o], fourier_b_out[o])
            if ctx_full:
                cx = cx + mc[5] * fourier_mixer(modulate(cx, mc[3], mc[4]), fourier_w_out[o], fourier_b_out[o])
        x = x + m[5] * y
        x = x + 0.5 * m[8] * swiglu(modulate(x, m[6], m[7]), ffn_w_gu[l, 1], ffn_w_down[l, 1])
        if ctx_full:
            cx = cx + 0.5 * mc[8] * swiglu(modulate(cx, mc[6], mc[7]), ffn_w_gu[l, 1], ffn_w_down[l, 1])
    return rmsnorm(x, final_norm_g)
```

```python
import functools
import math

import numpy as np
import jax
import jax.numpy as jnp
from jax import lax
from jax.experimental import pallas as pl
from jax.experimental.pallas import tpu as pltpu

D = 1024
DEPTH = 2
N_MOD = 9
D_FF = 2816
EPS = 1e-6
GRID_W = 64
HEADS = 4
HEAD_W = 128
COMP = 64
ROPE_FREQS = 16
ROPE_BASE = 10000.0
QKV_W = HEADS * HEAD_W
GMLP_W = 512
GMLP_CHUNK = 128
IN_COLS = 3 * QKV_W + 2 * GMLP_W
FOURIER_GROUP_CH = 256
DFT_R = 64
SUB = 8

VMEM_LIMIT = 56 * 1024 * 1024
FF_CHUNK = 256
TOKEN_TILE = 512

BF16 = jnp.bfloat16
F32 = jnp.float32


def _const_spec(shape):
    zeros = (0,) * len(shape)
    return pl.BlockSpec(shape, lambda *_: zeros, pipeline_mode=pl.Buffered(1))


def _rms(x):
    return x * lax.rsqrt(jnp.mean(x * x, axis=-1, keepdims=True) + EPS)


def _dot(a, b):
    return jnp.dot(a, b, preferred_element_type=F32)


def _ada_kernel(c_ref, w_ref, b_ref, o_ref):
    s = jax.nn.silu(c_ref[...]).astype(BF16)
    o_ref[0] = _dot(s, w_ref[0].astype(BF16)) + b_ref[0]


def _ada(cc, ada_w, ada_b):
    rows = cc.shape[0]
    tn = 1024
    return pl.pallas_call(
        _ada_kernel,
        grid=(DEPTH, N_MOD * D // tn),
        in_specs=[
            pl.BlockSpec((rows, D), lambda l, j: (0, 0)),
            pl.BlockSpec((1, D, tn), lambda l, j: (l, 0, j)),
            pl.BlockSpec((1, 1, tn), lambda l, j: (l, 0, j)),
        ],
        out_specs=pl.BlockSpec((1, rows, tn), lambda l, j: (l, 0, j)),
        out_shape=jax.ShapeDtypeStruct((DEPTH, rows, N_MOD * D), F32),
        name="ada",
    )(cc, ada_w, ada_b.reshape(DEPTH, 1, N_MOD * D))


def _ffn_kernel(*refs, n_mix, mix_bias, mod_idx, emit_idx, final_norm):
    refs = list(refs)
    x_ref, mod_ref = refs[:2]
    pos = 2
    mix_refs = refs[pos:pos + 2 * n_mix]
    pos += 2 * n_mix
    bmix_ref = None
    if mix_bias:
        bmix_ref = refs[pos]
        pos += 1
    wgu_ref, wd_ref = refs[pos:pos + 2]
    pos += 2
    fng_ref = None
    if final_norm:
        fng_ref = refs[pos]
        pos += 1
    out_ref = refs[pos]
    pos += 1
    hn_ref = None
    if emit_idx is not None:
        hn_ref = refs[pos]
        pos += 1
    h_scr, acc_scr = refs[pos:pos + 2]

    def mod(i):
        return mod_ref[0, i:i + 1, :]

    x = x_ref[0]
    if n_mix:
        y = None
        for m in range(n_mix):
            t = _dot(mix_refs[2 * m][0].astype(BF16), mix_refs[2 * m + 1][...])
            y = t if y is None else y + t
        if mix_bias:
            y = y + bmix_ref[...]
        x = x + mod(5) * y

    sh, sc, gt = mod_idx
    h_scr[...] = (_rms(x) * (1.0 + mod(sc)) + mod(sh)).astype(BF16)
    acc_scr[...] = jnp.zeros_like(acc_scr)

    def step(f, carry):
        gu = _dot(h_scr[...], wgu_ref[f])
        g = gu[:, :FF_CHUNK]
        u = gu[:, FF_CHUNK:]
        a = (jax.nn.silu(g) * u).astype(BF16)
        acc_scr[...] += _dot(a, wd_ref[f])
        return carry

    lax.fori_loop(0, D_FF // FF_CHUNK, step, 0)

    out = x + (0.5 * mod(gt)) * acc_scr[...]
    if emit_idx is not None:
        esh, esc = emit_idx
        hn_ref[0] = (_rms(out) * (1.0 + mod(esc)) + mod(esh)).astype(hn_ref.dtype)
    if final_norm:
        out = _rms(out) * fng_ref[...]
    out_ref[0] = out


def _ffn(x, mod_l, mod_row, wgu, wd, *, mod_idx, mixes=(), mix_bias=None,
         emit_idx=None, emit_dtype=BF16, final_g=None):
    nb, nt, _ = x.shape
    tm = min(TOKEN_TILE, nt)
    tok = lambda b, t: (b, t, 0)
    in_specs = [
        pl.BlockSpec((1, tm, D), tok),
        pl.BlockSpec((1, N_MOD, D), lambda b, t: (mod_row(b), 0, 0)),
    ]
    args = [x, mod_l]
    for arr, w in mixes:
        in_specs += [pl.BlockSpec((1, tm, arr.shape[-1]), tok), _const_spec(w.shape)]
        args += [arr, w]
    if mix_bias is not None:
        in_specs.append(_const_spec((1, D)))
        args.append(mix_bias.reshape(1, D))
    in_specs += [_const_spec(wgu.shape), _const_spec(wd.shape)]
    args += [wgu, wd]
    if final_g is not None:
        in_specs.append(_const_spec((1, D)))
        args.append(final_g.reshape(1, D))
    out_specs = [pl.BlockSpec((1, tm, D), tok)]
    out_shape = [jax.ShapeDtypeStruct(x.shape, F32)]
    if emit_idx is not None:
        out_specs.append(pl.BlockSpec((1, tm, D), tok))
        out_shape.append(jax.ShapeDtypeStruct(x.shape, emit_dtype))
    kern = functools.partial(
        _ffn_kernel, n_mix=len(mixes), mix_bias=mix_bias is not None, mod_idx=mod_idx,
        emit_idx=emit_idx, final_norm=final_g is not None)
    res = pl.pallas_call(
        kern,
        grid=(nb, nt // tm),
        in_specs=in_specs,
        out_specs=out_specs,
        out_shape=out_shape,
        scratch_shapes=[pltpu.VMEM((tm, D), BF16), pltpu.VMEM((tm, D), F32)],
        compiler_params=pltpu.CompilerParams(
            dimension_semantics=("parallel", "parallel"), vmem_limit_bytes=VMEM_LIMIT),
        name="ffn",
    )(*args)
    return res if emit_idx is not None else res[0]


def _prep_ffn_weights(w_gu, w_down):
    nf = D_FF // FF_CHUNK
    g = w_gu[:, :D_FF].reshape(D, nf, FF_CHUNK)
    u = w_gu[:, D_FF:].reshape(D, nf, FF_CHUNK)
    wgu = jnp.concatenate([g, u], axis=-1).transpose(1, 0, 2).astype(BF16)
    wd = w_down.reshape(nf, FF_CHUNK, D).astype(BF16)
    return wgu, wd


def _inproj_kernel(h_ref, w_ref, cos_ref, sa_ref, sb_ref, ng_ref, nb_ref, ws_ref, bs_ref,
                   q_ref, k_ref, v_ref, g_ref):
    h = h_ref[0]
    tm = h.shape[0]

    def rope(x, scale):
        up = pltpu.roll(x, HEAD_W - ROPE_FREQS, 1)
        dn = pltpu.roll(x, ROPE_FREQS, 1)
        r = x * cos_ref[...] + up * sa_ref[...] + dn * sb_ref[...]
        return r * scale if scale != 1.0 else r

    for name, ref, col0, scale in (("q", q_ref, 0, COMP ** -0.5), ("k", k_ref, QKV_W, 1.0)):
        p = _dot(h, w_ref[:, col0:col0 + QKV_W])
        for hd in range(HEADS):
            sl = slice(hd * HEAD_W, (hd + 1) * HEAD_W)
            ref[0, :, sl] = rope(p[:, sl], scale).astype(BF16)
    v_ref[0] = _dot(h, w_ref[:, 2 * QKV_W:3 * QKV_W]).astype(BF16)

    z = jax.nn.gelu(_dot(h, w_ref[:, 3 * QKV_W:]), approximate=True)
    u = z[:, :GMLP_W]
    vg = z[:, GMLP_W:]
    for grp in range(GMLP_W // GMLP_CHUNK):
        sl = slice(grp * GMLP_CHUNK, (grp + 1) * GMLP_CHUNK)
        t = vg[:, sl]
        mu = jnp.mean(t, axis=-1, keepdims=True)
        tc = t - mu
        var = jnp.mean(tc * tc, axis=-1, keepdims=True)
        tn = (tc * lax.rsqrt(var + EPS) * ng_ref[:, sl] + nb_ref[:, sl]).astype(BF16)
        for ch in range(tm // GMLP_CHUNK):
            rows = slice(ch * GMLP_CHUNK, (ch + 1) * GMLP_CHUNK)
            mixed = _dot(ws_ref[grp], tn[rows]) + bs_ref[:, sl]
            g_ref[0, rows, sl] = (u[rows, sl] * mixed).astype(BF16)


def _inproj(h, w_in, cos_t, sa_t, sb_t, norm_g, norm_b, w_s, b_s):
    nb, nt, _ = h.shape
    tm = TOKEN_TILE
    tok = lambda b, t: (b, t, 0)
    tab = pl.BlockSpec((tm, HEAD_W), lambda b, t: (t, 0))
    out = jax.ShapeDtypeStruct((nb, nt, QKV_W), BF16)
    return pl.pallas_call(
        _inproj_kernel,
        grid=(nb, nt // tm),
        in_specs=[pl.BlockSpec((1, tm, D), tok), _const_spec(w_in.shape), tab, tab, tab,
                  _const_spec(norm_g.shape), _const_spec(norm_b.shape),
                  _const_spec(w_s.shape), _const_spec(b_s.shape)],
        out_specs=[pl.BlockSpec((1, tm, QKV_W), tok)] * 4,
        out_shape=[out] * 4,
        compiler_params=pltpu.CompilerParams(
            dimension_semantics=("parallel", "parallel"), vmem_limit_bytes=VMEM_LIMIT),
        name="inproj",
    )(h, w_in, cos_t, sa_t, sb_t, norm_g, norm_b, w_s, b_s)


def _ctx_kv_kernel(h_ref, w_ref, k_ref, v_ref):
    h = h_ref[0]
    k_ref[0] = _dot(h, w_ref[:, QKV_W:2 * QKV_W]).astype(BF16)
    v_ref[0] = _dot(h, w_ref[:, 2 * QKV_W:3 * QKV_W]).astype(BF16)


def _ctx_kv(h, w_in):
    nb, nt, _ = h.shape
    blk = lambda b: (b, 0, 0)
    out = jax.ShapeDtypeStruct((nb, nt, QKV_W), BF16)
    return pl.pallas_call(
        _ctx_kv_kernel,
        grid=(nb,),
        in_specs=[pl.BlockSpec((1, nt, D), blk), _const_spec(w_in.shape)],
        out_specs=[pl.BlockSpec((1, nt, QKV_W), blk)] * 2,
        out_shape=[out] * 2,
        compiler_params=pltpu.CompilerParams(vmem_limit_bytes=VMEM_LIMIT),
        name="ctx_kv",
    )(h, w_in)


def _rope_tables(seq):
    rows = seq // GRID_W
    row = jnp.repeat(jnp.arange(rows), GRID_W)
    col = jnp.tile(jnp.arange(GRID_W), rows)
    inv = ROPE_BASE ** (-jnp.arange(ROPE_FREQS, dtype=F32) / ROPE_FREQS)
    ang = jnp.stack([row[:, None] * inv, col[:, None] * inv], axis=1)
    cos, sin = jnp.cos(ang), jnp.sin(ang)
    zero = jnp.zeros_like(sin)
    cos_c = jnp.stack([cos, cos], axis=2).reshape(seq, COMP)
    sa_c = jnp.stack([-sin, zero], axis=2).reshape(seq, COMP)
    sb_c = jnp.stack([zero, sin], axis=2).reshape(seq, COMP)
    two = lambda t: jnp.concatenate([t, t], axis=-1)
    return two(cos_c), two(sa_c), two(sb_c)


def _attn_kernel(q_ref, k_ref, v_ref, lam_ref, g_ref, o_ref, *, lam_init):
    lv = lam_ref[...]
    lam = (jnp.exp(jnp.sum(lv[0:1] * lv[1:2], axis=-1, keepdims=True))
           - jnp.exp(jnp.sum(lv[2:3] * lv[3:4], axis=-1, keepdims=True)) + lam_init)
    tq = q_ref.shape[1]
    lane = lax.broadcasted_iota(jnp.int32, (tq, HEAD_W), 1)
    nt = (((1,), (1,)), ((), ()))
    for hd in range(HEADS):
        sl = slice(hd * HEAD_W, (hd + 1) * HEAD_W)
        qh = q_ref[0, :, sl]
        zero = jnp.zeros_like(qh)
        qq = jnp.concatenate([jnp.where(lane < COMP, qh, zero),
                              jnp.where(lane >= COMP, qh, zero)], axis=0)
        s = lax.dot_general(qq, k_ref[0, :, sl], nt, preferred_element_type=F32)
        e = jnp.exp(s - jnp.max(s, axis=-1, keepdims=True))
        r = 1.0 / jnp.sum(e, axis=-1, keepdims=True)
        a = e[:tq] * r[:tq] - e[tq:] * (lam * r[tq:])
        o = _dot(a.astype(BF16), v_ref[0, :, sl])
        y = _rms(o) * (g_ref[...] * (1.0 - lam_init))
        o_ref[0, :, sl] = y.astype(BF16)


def _attn(q, k_all, v_all, lam_vec, subln_g, lam_init, tq=256):
    nb, nt, _ = q.shape
    nk = k_all.shape[1]
    kv = pl.BlockSpec((1, nk, QKV_W), lambda b, t: (b, 0, 0))
    return pl.pallas_call(
        functools.partial(_attn_kernel, lam_init=lam_init),
        grid=(nb, nt // tq),
        in_specs=[pl.BlockSpec((1, tq, QKV_W), lambda b, t: (b, t, 0)), kv, kv,
                  _const_spec(lam_vec.shape), _const_spec((1, HEAD_W))],
        out_specs=pl.BlockSpec((1, tq, QKV_W), lambda b, t: (b, t, 0)),
        out_shape=jax.ShapeDtypeStruct((nb, nt, QKV_W), BF16),
        compiler_params=pltpu.CompilerParams(
            dimension_semantics=("parallel", "parallel"), vmem_limit_bytes=VMEM_LIMIT),
        name="attn",
    )(q, k_all, v_all, lam_vec, subln_g.reshape(1, HEAD_W))


def _fourier_consts():
    r, s, gc = DFT_R, SUB, FOURIER_GROUP_CH
    idx = np.arange(r)
    a64 = 2.0 * np.pi * np.outer(idx, idx) / r
    c64, s64 = np.cos(a64), np.sin(a64)
    eye = np.eye(s)
    ach = 2.0 * np.pi * np.outer(np.arange(gc), np.arange(gc)) / gc
    scale = 1.0 / math.sqrt(r * r * gc)
    cd = np.concatenate([np.cos(ach), -np.sin(ach)], axis=1) * scale
    q = np.block([[np.kron(c64, eye), np.kron(s64, eye)],
                  [np.kron(-s64, eye), np.kron(c64, eye)]])
    n2 = np.arange(r).reshape(s, 1, s)
    k1 = idx.reshape(1, r, 1)
    at = 2.0 * np.pi * (n2 * k1) / (r * r)
    tw = lambda t: np.broadcast_to(t.reshape(s, r * s, 1), (s, r * s, 128))
    pc = np.einsum("kgn,ab->kagbn", c64.reshape(r, s, s), eye).reshape(r * s, r * s)
    ps = np.einsum("kgn,ab->kagbn", s64.reshape(r, s, s), eye).reshape(r * s, r * s)
    p = np.concatenate([pc, ps], axis=1)
    bf = lambda t: jnp.asarray(t, dtype=F32).astype(BF16)
    return (bf(cd), bf(q), jnp.asarray(tw(np.cos(at)), F32), jnp.asarray(tw(np.sin(at)), F32), bf(p))


def _fourier_kernel(h_ref, cd_ref, q_ref, tc_ref, ts_ref, p_ref, o_ref, t_scr):
    r, s, gc = DFT_R, SUB, FOURIER_GROUP_CH
    rows = r * s
    wide = lambda t: jnp.concatenate([t, t], axis=1)
    for g in range(s):
        hg = h_ref[0, :, g, :, :].reshape(rows, gc).astype(BF16)
        z = _dot(hg, cd_ref[...])
        zs = jnp.concatenate([z[:, :gc], z[:, gc:]], axis=0).astype(BF16)
        y = _dot(q_ref[...], zs)
        yr, yi = y[:rows], y[rows:]
        tc, ts = wide(tc_ref[g]), wide(ts_ref[g])
        t_scr[0, :, g] = (yr * tc + yi * ts).astype(BF16).reshape(s, r, gc)
        t_scr[1, :, g] = (yi * tc - yr * ts).astype(BF16).reshape(s, r, gc)
    for kg in range(s):
        t = jnp.concatenate([t_scr[0, kg].reshape(rows, gc), t_scr[1, kg].reshape(rows, gc)], axis=0)
        o_ref[0, :, kg, :, :] = _dot(p_ref[...], t).reshape(r, s, gc)


def _fourier(h, consts):
    nb, seq, _ = h.shape
    r, s, gc = DFT_R, SUB, FOURIER_GROUP_CH
    cd, q, tc, ts, p = consts
    h5 = h.reshape(nb, r, s, s, D)
    blk = pl.BlockSpec((1, r, s, s, gc), lambda b, c: (b, 0, 0, 0, c))
    out = pl.pallas_call(
        _fourier_kernel,
        grid=(nb, D // gc),
        in_specs=[blk, _const_spec(cd.shape), _const_spec(q.shape), _const_spec(tc.shape),
                  _const_spec(ts.shape), _const_spec(p.shape)],
        out_specs=blk,
        out_shape=jax.ShapeDtypeStruct(h5.shape, F32),
        scratch_shapes=[pltpu.VMEM((2, s, s, r, gc), BF16)],
        compiler_params=pltpu.CompilerParams(
            dimension_semantics=("parallel", "parallel"), vmem_limit_bytes=VMEM_LIMIT),
        name="fourier",
    )(h5, cd, q, tc, ts, p)
    return out.reshape(nb, seq, D)


def kernel(x, c, ctx, c_ctx, ada_w, ada_b, ffn_w_gu, ffn_w_down, mix_w_in, mix_w_out, diff_lambda,
           diff_subln_g, sgu_norm_g, sgu_norm_b, sgu_w, sgu_b, fourier_w_out, fourier_b_out, final_norm_g):
    nb, seq, _ = x.shape
    assert seq == DFT_R * DFT_R and DEPTH == 2
    ctx_row = nb
    cc = jnp.concatenate([c, c_ctx[None, :], jnp.zeros((2 * SUB - nb - 1, D), F32)], axis=0)
    mods = _ada(cc, ada_w, ada_b).reshape(DEPTH, cc.shape[0], N_MOD, D)
    latent = lambda b: b
    shared = lambda b: ctx_row

    ffn_w = [[_prep_ffn_weights(ffn_w_gu[l, i], ffn_w_down[l, i]) for i in range(2)] for l in range(DEPTH)]

    x, h = _ffn(x, mods[0], latent, *ffn_w[0][0], mod_idx=(0, 1, 2), emit_idx=(3, 4))
    _, hc = _ffn(ctx, mods[0], shared, *ffn_w[0][0], mod_idx=(0, 1, 2), emit_idx=(3, 4))

    w_in = mix_w_in[0].astype(BF16)
    cos_t, sa_t, sb_t = _rope_tables(seq)
    bs_rows = jnp.repeat(sgu_b[0].T, GMLP_CHUNK, axis=1)
    q, k, v, gated = _inproj(h, w_in, cos_t, sa_t, sb_t,
                             sgu_norm_g[0].reshape(1, GMLP_W), sgu_norm_b[0].reshape(1, GMLP_W),
                             sgu_w[0].astype(BF16), bs_rows)
    kc, vc = _ctx_kv(hc, w_in)
    lam_init = 0.8 - 0.6 * math.exp(-0.3 * 0)
    att = _attn(q, jnp.concatenate([k, kc], axis=1), jnp.concatenate([v, vc], axis=1),
                diff_lambda[0], diff_subln_g[0], lam_init)
    w_out = mix_w_out[0].astype(BF16)
    x = _ffn(x, mods[0], latent, *ffn_w[0][1], mod_idx=(6, 7, 8),
             mixes=((att, w_out[:QKV_W]), (gated, w_out[QKV_W:])))

    x, h = _ffn(x, mods[1], latent, *ffn_w[1][0], mod_idx=(0, 1, 2), emit_idx=(3, 4), emit_dtype=F32)
    f = _fourier(h, _fourier_consts())
    return _ffn(x, mods[1], latent, *ffn_w[1][1], mod_idx=(6, 7, 8),
                mixes=((f, fourier_w_out[0].astype(BF16)),), mix_bias=fourier_b_out[0],
                final_g=final_norm_g)
```

```python
import functools
import math

import numpy as np
import jax
import jax.numpy as jnp
from jax import lax
from jax.experimental import pallas as pl
from jax.experimental.pallas import tpu as pltpu

D = 1024
DEPTH = 2
N_MOD = 9
D_FF = 2816
EPS = 1e-6
GRID_W = 64
HEADS = 4
HEAD_W = 128
COMP = 64
ROPE_FREQS = 16
ROPE_BASE = 10000.0
QKV_W = HEADS * HEAD_W
GMLP_W = 512
GMLP_CHUNK = 128
IN_COLS = 3 * QKV_W + 2 * GMLP_W
FOURIER_GROUP_CH = 256
DFT_R = 64
SUB = 8

VMEM_LIMIT = 56 * 1024 * 1024
FF_CHUNK = 256
TOKEN_TILE = 512
ATTN_Q_TILE = 1024
KEY_CHUNK = 512
ONES_ROWS = 16
Q_STREAM = 128
Q_SCALE = COMP ** -0.5 * math.log2(math.e)

BF16 = jnp.bfloat16
F32 = jnp.float32


def _const_spec(shape):
    zeros = (0,) * len(shape)
    return pl.BlockSpec(shape, lambda *_: zeros, pipeline_mode=pl.Buffered(1))


def _rms(x):
    return x * lax.rsqrt(jnp.mean(x * x, axis=-1, keepdims=True) + EPS)


def _dot(a, b):
    return jnp.dot(a, b, preferred_element_type=F32)


def _ada_kernel(c_ref, w_ref, b_ref, o_ref):
    s = jax.nn.silu(c_ref[...]).astype(BF16)
    o_ref[0] = _dot(s, w_ref[0].astype(BF16)) + b_ref[0]


def _ada(cc, ada_w, ada_b):
    rows = cc.shape[0]
    tn = 1024
    return pl.pallas_call(
        _ada_kernel,
        grid=(DEPTH, N_MOD * D // tn),
        in_specs=[
            pl.BlockSpec((rows, D), lambda l, j: (0, 0)),
            pl.BlockSpec((1, D, tn), lambda l, j: (l, 0, j)),
            pl.BlockSpec((1, 1, tn), lambda l, j: (l, 0, j)),
        ],
        out_specs=pl.BlockSpec((1, rows, tn), lambda l, j: (l, 0, j)),
        out_shape=jax.ShapeDtypeStruct((DEPTH, rows, N_MOD * D), F32),
        name="ada",
    )(cc, ada_w, ada_b.reshape(DEPTH, 1, N_MOD * D))


def _ffn_kernel(*refs, n_mix, mix_bias, mod_idx, emit_idx, final_norm):
    refs = list(refs)
    x_ref, mod_ref = refs[:2]
    pos = 2
    mix_refs = refs[pos:pos + 2 * n_mix]
    pos += 2 * n_mix
    bmix_ref = None
    if mix_bias:
        bmix_ref = refs[pos]
        pos += 1
    wgu_ref, wd_ref = refs[pos:pos + 2]
    pos += 2
    fng_ref = None
    if final_norm:
        fng_ref = refs[pos]
        pos += 1
    out_ref = refs[pos]
    pos += 1
    hn_ref = None
    if emit_idx is not None:
        hn_ref = refs[pos]

    def mod(i):
        return mod_ref[0, i:i + 1, :]

    x = x_ref[0]
    if n_mix:
        y = None
        for m in range(n_mix):
            t = _dot(mix_refs[2 * m][0].astype(BF16), mix_refs[2 * m + 1][...])
            y = t if y is None else y + t
        if mix_bias:
            y = y + bmix_ref[...]
        x = x + mod(5) * y

    sh, sc, gt = mod_idx
    h = (_rms(x) * (1.0 + mod(sc)) + mod(sh)).astype(BF16)
    acc = None
    for f in range(D_FF // FF_CHUNK):
        gu = _dot(h, wgu_ref[f])
        a = (jax.nn.silu(gu[:, :FF_CHUNK]) * gu[:, FF_CHUNK:]).astype(BF16)
        t = _dot(a, wd_ref[f])
        acc = t if acc is None else acc + t

    out = x + (0.5 * mod(gt)) * acc
    if emit_idx is not None:
        esh, esc = emit_idx
        hn_ref[0] = (_rms(out) * (1.0 + mod(esc)) + mod(esh)).astype(hn_ref.dtype)
    if final_norm:
        out = _rms(out) * fng_ref[...]
    out_ref[0] = out


def _ffn(x, mod_l, mod_row, wgu, wd, *, mod_idx, mixes=(), mix_bias=None,
         emit_idx=None, emit_dtype=BF16, final_g=None):
    nb, nt, _ = x.shape
    tm = min(TOKEN_TILE, nt)
    tok = lambda b, t: (b, t, 0)
    in_specs = [
        pl.BlockSpec((1, tm, D), tok),
        pl.BlockSpec((1, N_MOD, D), lambda b, t: (mod_row(b), 0, 0)),
    ]
    args = [x, mod_l]
    for arr, w in mixes:
        in_specs += [pl.BlockSpec((1, tm, arr.shape[-1]), tok), _const_spec(w.shape)]
        args += [arr, w]
    if mix_bias is not None:
        in_specs.append(_const_spec((1, D)))
        args.append(mix_bias.reshape(1, D))
    in_specs += [_const_spec(wgu.shape), _const_spec(wd.shape)]
    args += [wgu, wd]
    if final_g is not None:
        in_specs.append(_const_spec((1, D)))
        args.append(final_g.reshape(1, D))
    out_specs = [pl.BlockSpec((1, tm, D), tok)]
    out_shape = [jax.ShapeDtypeStruct(x.shape, F32)]
    if emit_idx is not None:
        out_specs.append(pl.BlockSpec((1, tm, D), tok))
        out_shape.append(jax.ShapeDtypeStruct(x.shape, emit_dtype))
    kern = functools.partial(
        _ffn_kernel, n_mix=len(mixes), mix_bias=mix_bias is not None, mod_idx=mod_idx,
        emit_idx=emit_idx, final_norm=final_g is not None)
    res = pl.pallas_call(
        kern,
        grid=(nb, nt // tm),
        in_specs=in_specs,
        out_specs=out_specs,
        out_shape=out_shape,
        compiler_params=pltpu.CompilerParams(
            dimension_semantics=("parallel", "parallel"), vmem_limit_bytes=VMEM_LIMIT),
        name="ffn",
    )(*args)
    return res if emit_idx is not None else res[0]


def _prep_ffn_weights(w_gu, w_down):
    nf = D_FF // FF_CHUNK
    g = w_gu[:, :D_FF].reshape(D, nf, FF_CHUNK)
    u = w_gu[:, D_FF:].reshape(D, nf, FF_CHUNK)
    wgu = jnp.concatenate([g, u], axis=-1).transpose(1, 0, 2).astype(BF16)
    wd = w_down.reshape(nf, FF_CHUNK, D).astype(BF16)
    return wgu, wd


def _inproj_kernel(h_ref, w_ref, cos_ref, sa_ref, sb_ref, ng_ref, nb_ref, ws_ref, bs_ref,
                   q_ref, k_ref, v_ref, g_ref):
    h = h_ref[0]
    tm = h.shape[0]

    def rope(x, scale):
        up = pltpu.roll(x, HEAD_W - ROPE_FREQS, 1)
        dn = pltpu.roll(x, ROPE_FREQS, 1)
        r = x * cos_ref[...] + up * sa_ref[...] + dn * sb_ref[...]
        return r * scale if scale != 1.0 else r

    for ref, col0, scale in ((q_ref, 0, Q_SCALE), (k_ref, QKV_W, 1.0)):
        p = _dot(h, w_ref[:, col0:col0 + QKV_W])
        for hd in range(HEADS):
            sl = slice(hd * HEAD_W, (hd + 1) * HEAD_W)
            ref[0, :, sl] = rope(p[:, sl], scale).astype(BF16)
    v_ref[0] = _dot(h, w_ref[:, 2 * QKV_W:3 * QKV_W]).T.astype(BF16)

    z = jax.nn.gelu(_dot(h, w_ref[:, 3 * QKV_W:]), approximate=True)
    u = z[:, :GMLP_W]
    vg = z[:, GMLP_W:]
    for grp in range(GMLP_W // GMLP_CHUNK):
        sl = slice(grp * GMLP_CHUNK, (grp + 1) * GMLP_CHUNK)
        t = vg[:, sl]
        mu = jnp.mean(t, axis=-1, keepdims=True)
        tc = t - mu
        var = jnp.mean(tc * tc, axis=-1, keepdims=True)
        tn = (tc * lax.rsqrt(var + EPS) * ng_ref[:, sl] + nb_ref[:, sl]).astype(BF16)
        for ch in range(tm // GMLP_CHUNK):
            rows = slice(ch * GMLP_CHUNK, (ch + 1) * GMLP_CHUNK)
            mixed = _dot(ws_ref[grp], tn[rows]) + bs_ref[:, sl]
            g_ref[0, rows, sl] = (u[rows, sl] * mixed).astype(BF16)


def _inproj(h, w_in, cos_t, sa_t, sb_t, norm_g, norm_b, w_s, b_s):
    nb, nt, _ = h.shape
    tm = TOKEN_TILE
    tok = lambda b, t: (b, t, 0)
    tab = pl.BlockSpec((tm, HEAD_W), lambda b, t: (t, 0))
    out = jax.ShapeDtypeStruct((nb, nt, QKV_W), BF16)
    out_t = jax.ShapeDtypeStruct((nb, QKV_W, nt), BF16)
    tok_spec = pl.BlockSpec((1, tm, QKV_W), tok)
    return pl.pallas_call(
        _inproj_kernel,
        grid=(nb, nt // tm),
        in_specs=[pl.BlockSpec((1, tm, D), tok), _const_spec(w_in.shape), tab, tab, tab,
                  _const_spec(norm_g.shape), _const_spec(norm_b.shape),
                  _const_spec(w_s.shape), _const_spec(b_s.shape)],
        out_specs=[tok_spec, tok_spec, pl.BlockSpec((1, QKV_W, tm), lambda b, t: (b, 0, t)), tok_spec],
        out_shape=[out, out, out_t, out],
        compiler_params=pltpu.CompilerParams(
            dimension_semantics=("parallel", "parallel"), vmem_limit_bytes=VMEM_LIMIT),
        name="inproj",
    )(h, w_in, cos_t, sa_t, sb_t, norm_g, norm_b, w_s, b_s)


def _ctx_kv_kernel(h_ref, w_ref, k_ref, v_ref):
    h = h_ref[0]
    k_ref[0] = _dot(h, w_ref[:, QKV_W:2 * QKV_W]).astype(BF16)
    v_ref[0] = _dot(h, w_ref[:, 2 * QKV_W:3 * QKV_W]).T.astype(BF16)


def _ctx_kv(h, w_in):
    nb, nt, _ = h.shape
    blk = lambda b: (b, 0, 0)
    return pl.pallas_call(
        _ctx_kv_kernel,
        grid=(nb,),
        in_specs=[pl.BlockSpec((1, nt, D), blk), _const_spec(w_in.shape)],
        out_specs=[pl.BlockSpec((1, nt, QKV_W), blk), pl.BlockSpec((1, QKV_W, nt), blk)],
        out_shape=[jax.ShapeDtypeStruct((nb, nt, QKV_W), BF16), jax.ShapeDtypeStruct((nb, QKV_W, nt), BF16)],
        compiler_params=pltpu.CompilerParams(vmem_limit_bytes=VMEM_LIMIT),
        name="ctx_kv",
    )(h, w_in)


def _rope_tables(seq):
    rows = seq // GRID_W
    row = jnp.repeat(jnp.arange(rows), GRID_W)
    col = jnp.tile(jnp.arange(GRID_W), rows)
    inv = ROPE_BASE ** (-jnp.arange(ROPE_FREQS, dtype=F32) / ROPE_FREQS)
    ang = jnp.stack([row[:, None] * inv, col[:, None] * inv], axis=1)
    cos, sin = jnp.cos(ang), jnp.sin(ang)
    zero = jnp.zeros_like(sin)
    cos_c = jnp.stack([cos, cos], axis=2).reshape(seq, COMP)
    sa_c = jnp.stack([-sin, zero], axis=2).reshape(seq, COMP)
    sb_c = jnp.stack([zero, sin], axis=2).reshape(seq, COMP)
    two = lambda t: jnp.concatenate([t, t], axis=-1)
    return two(cos_c), two(sa_c), two(sb_c)


def _attn_kernel(q_ref, k_ref, vt_ref, lam_ref, g_ref, o_ref, *, lam_init):
    lv = lam_ref[...]
    lam = (jnp.exp(jnp.sum(lv[0:1] * lv[1:2], axis=-1, keepdims=True))
           - jnp.exp(jnp.sum(lv[2:3] * lv[3:4], axis=-1, keepdims=True)) + lam_init)
    tq = q_ref.shape[1]
    nk = k_ref.shape[1]
    sq = Q_STREAM
    nt_dims = (((1,), (1,)), ((), ()))
    chunks = [slice(k0, min(k0 + KEY_CHUNK, nk)) for k0 in range(0, nk, KEY_CHUNK)]
    lane = lax.broadcasted_iota(jnp.int32, (sq, HEAD_W), 1)

    qq = []
    for j in range(tq // sq):
        qh = q_ref[0, j * sq:(j + 1) * sq, :]
        zero = jnp.zeros_like(qh)
        qq.append(jnp.concatenate([jnp.where(lane < COMP, qh, zero),
                                   jnp.where(lane >= COMP, qh, zero)], axis=0))

    def scores(rows, j):
        return lax.dot_general(k_ref[0, rows, :], qq[j], nt_dims, preferred_element_type=F32)

    n_streams = tq // sq
    m = [jnp.full((1, 2 * sq), -1e30, F32)] * n_streams
    acc = [jnp.zeros((HEAD_W + ONES_ROWS, 2 * sq), F32)] * n_streams
    s_cur = [scores(chunks[0], j) for j in range(n_streams)]
    for c, rows in enumerate(chunks):
        ones = jnp.ones((ONES_ROWS, rows.stop - rows.start), BF16)
        vt = jnp.concatenate([vt_ref[0, :, rows], ones], axis=0)
        for j in range(n_streams):
            s = s_cur[j]
            if c + 1 < len(chunks):
                s_cur[j] = scores(chunks[c + 1], j)
            m_new = jnp.maximum(m[j], jnp.max(s, axis=0, keepdims=True))
            e = jnp.exp2((s - m_new).astype(BF16))
            acc[j] = acc[j] * jnp.exp2(m[j] - m_new) + _dot(vt, e)
            m[j] = m_new
    for j in range(n_streams):
        r = 1.0 / acc[j][HEAD_W:HEAD_W + 1]
        o_t = acc[j][:HEAD_W, :sq] * r[:, :sq] - acc[j][:HEAD_W, sq:] * (lam * r[:, sq:])
        y = _rms(o_t.T) * (g_ref[...] * (1.0 - lam_init))
        o_ref[0, j * sq:(j + 1) * sq, :] = y.astype(BF16)


def _attn(q, k_all, vt_all, lam_vec, subln_g, lam_init):
    nb, nt, _ = q.shape
    nk = k_all.shape[1]
    tq = ATTN_Q_TILE
    assert nk % 256 == 0 and nt % tq == 0
    return pl.pallas_call(
        functools.partial(_attn_kernel, lam_init=lam_init),
        grid=(nb, HEADS, nt // tq),
        in_specs=[pl.BlockSpec((1, tq, HEAD_W), lambda b, h, t: (b, t, h)),
                  pl.BlockSpec((1, nk, HEAD_W), lambda b, h, t: (b, 0, h)),
                  pl.BlockSpec((1, HEAD_W, nk), lambda b, h, t: (b, h, 0)),
                  _const_spec(lam_vec.shape), _const_spec((1, HEAD_W))],
        out_specs=pl.BlockSpec((1, tq, HEAD_W), lambda b, h, t: (b, t, h)),
        out_shape=jax.ShapeDtypeStruct((nb, nt, QKV_W), BF16),
        compiler_params=pltpu.CompilerParams(
            dimension_semantics=("parallel", "parallel", "parallel"), vmem_limit_bytes=VMEM_LIMIT),
        name="attn",
    )(q, k_all, vt_all, lam_vec, subln_g.reshape(1, HEAD_W))


def _fourier_consts():
    r, s, gc = DFT_R, SUB, FOURIER_GROUP_CH
    idx = np.arange(r)
    a64 = 2.0 * np.pi * np.outer(idx, idx) / r
    c64, s64 = np.cos(a64), np.sin(a64)
    eye = np.eye(s)
    ach = 2.0 * np.pi * np.outer(np.arange(gc), np.arange(gc)) / gc
    scale = 1.0 / math.sqrt(r * r * gc)
    cd = np.concatenate([np.cos(ach), -np.sin(ach)], axis=1) * scale
    q = np.block([[np.kron(c64, eye), np.kron(s64, eye)],
                  [np.kron(-s64, eye), np.kron(c64, eye)]])
    n2 = np.arange(r).reshape(s, 1, s)
    k1 = idx.reshape(1, r, 1)
    at = 2.0 * np.pi * (n2 * k1) / (r * r)
    tw = lambda t: np.broadcast_to(t.reshape(s, r * s, 1), (s, r * s, 128))
    pc = np.einsum("kgn,ab->kagbn", c64.reshape(r, s, s), eye).reshape(r * s, r * s)
    ps = np.einsum("kgn,ab->kagbn", s64.reshape(r, s, s), eye).reshape(r * s, r * s)
    p = np.concatenate([pc, ps], axis=1)
    bf = lambda t: jnp.asarray(t, dtype=F32).astype(BF16)
    return (bf(cd), bf(q), jnp.asarray(tw(np.cos(at)), F32), jnp.asarray(tw(np.sin(at)), F32), bf(p))


def _fourier_kernel(h_ref, cd_ref, q_ref, tc_ref, ts_ref, p_ref, o_ref, t_scr):
    r, s, gc = DFT_R, SUB, FOURIER_GROUP_CH
    rows = r * s
    wide = lambda t: jnp.concatenate([t, t], axis=1)
    for g in range(s):
        hg = h_ref[0, :, g, :, :].reshape(rows, gc).astype(BF16)
        z = _dot(hg, cd_ref[...])
        zs = jnp.concatenate([z[:, :gc], z[:, gc:]], axis=0).astype(BF16)
        y = _dot(q_ref[...], zs)
        yr, yi = y[:rows], y[rows:]
        tc, ts = wide(tc_ref[g]), wide(ts_ref[g])
        t_scr[0, :, g] = (yr * tc + yi * ts).astype(BF16).reshape(s, r, gc)
        t_scr[1, :, g] = (yi * tc - yr * ts).astype(BF16).reshape(s, r, gc)
    for kg in range(s):
        t = jnp.concatenate([t_scr[0, kg].reshape(rows, gc), t_scr[1, kg].reshape(rows, gc)], axis=0)
        o_ref[0, :, kg, :, :] = _dot(p_ref[...], t).reshape(r, s, gc)


def _fourier(h, consts):
    nb, seq, _ = h.shape
    r, s, gc = DFT_R, SUB, FOURIER_GROUP_CH
    cd, q, tc, ts, p = consts
    h5 = h.reshape(nb, r, s, s, D)
    blk = pl.BlockSpec((1, r, s, s, gc), lambda b, c: (b, 0, 0, 0, c))
    out = pl.pallas_call(
        _fourier_kernel,
        grid=(nb, D // gc),
        in_specs=[blk, _const_spec(cd.shape), _const_spec(q.shape), _const_spec(tc.shape),
                  _const_spec(ts.shape), _const_spec(p.shape)],
        out_specs=blk,
        out_shape=jax.ShapeDtypeStruct(h5.shape, F32),
        scratch_shapes=[pltpu.VMEM((2, s, s, r, gc), BF16)],
        compiler_params=pltpu.CompilerParams(
            dimension_semantics=("parallel", "parallel"), vmem_limit_bytes=VMEM_LIMIT),
        name="fourier",
    )(h5, cd, q, tc, ts, p)
    return out.reshape(nb, seq, D)


def kernel(x, c, ctx, c_ctx, ada_w, ada_b, ffn_w_gu, ffn_w_down, mix_w_in, mix_w_out, diff_lambda,
           diff_subln_g, sgu_norm_g, sgu_norm_b, sgu_w, sgu_b, fourier_w_out, fourier_b_out, final_norm_g):
    nb, seq, _ = x.shape
    assert seq == DFT_R * DFT_R and DEPTH == 2
    ctx_row = nb
    cc = jnp.concatenate([c, c_ctx[None, :], jnp.zeros((2 * SUB - nb - 1, D), F32)], axis=0)
    mods = _ada(cc, ada_w, ada_b).reshape(DEPTH, cc.shape[0], N_MOD, D)
    latent = lambda b: b
    shared = lambda b: ctx_row

    ffn_w = [[_prep_ffn_weights(ffn_w_gu[l, i], ffn_w_down[l, i]) for i in range(2)] for l in range(DEPTH)]

    x, h = _ffn(x, mods[0], latent, *ffn_w[0][0], mod_idx=(0, 1, 2), emit_idx=(3, 4))
    _, hc = _ffn(ctx, mods[0], shared, *ffn_w[0][0], mod_idx=(0, 1, 2), emit_idx=(3, 4))

    w_in = mix_w_in[0].astype(BF16)
    cos_t, sa_t, sb_t = _rope_tables(seq)
    bs_rows = jnp.repeat(sgu_b[0].T, GMLP_CHUNK, axis=1)
    q, k, vt, gated = _inproj(h, w_in, cos_t, sa_t, sb_t,
                             sgu_norm_g[0].reshape(1, GMLP_W), sgu_norm_b[0].reshape(1, GMLP_W),
                             sgu_w[0].astype(BF16), bs_rows)
    kc, vct = _ctx_kv(hc, w_in)
    lam_init = 0.8 - 0.6 * math.exp(-0.3 * 0)
    att = _attn(q, jnp.concatenate([k, kc], axis=1), jnp.concatenate([vt, vct], axis=2),
                diff_lambda[0], diff_subln_g[0], lam_init)
    w_out = mix_w_out[0].astype(BF16)
    x = _ffn(x, mods[0], latent, *ffn_w[0][1], mod_idx=(6, 7, 8),
             mixes=((att, w_out[:QKV_W]), (gated, w_out[QKV_W:])))

    x, h = _ffn(x, mods[1], latent, *ffn_w[1][0], mod_idx=(0, 1, 2), emit_idx=(3, 4), emit_dtype=F32)
    f = _fourier(h, _fourier_consts())
    return _ffn(x, mods[1], latent, *ffn_w[1][1], mod_idx=(6, 7, 8),
                mixes=((f, fourier_w_out[0].astype(BF16)),), mix_bias=fourier_b_out[0],
                final_g=final_norm_g)
```

```python
import functools
import math

import numpy as np
import jax
import jax.numpy as jnp
from jax import lax
from jax.experimental import pallas as pl
from jax.experimental.pallas import tpu as pltpu

D = 1024
DEPTH = 2
N_MOD = 9
D_FF = 2816
EPS = 1e-6
GRID_W = 64
HEADS = 4
HEAD_W = 128
COMP = 64
ROPE_FREQS = 16
ROPE_BASE = 10000.0
QKV_W = HEADS * HEAD_W
GMLP_W = 512
GMLP_CHUNK = 128
IN_COLS = 3 * QKV_W + 2 * GMLP_W
FOURIER_GROUP_CH = 256
DFT_R = 64
SUB = 8

VMEM_LIMIT = 56 * 1024 * 1024
FF_CHUNK = 256
TOKEN_TILE = 512
ATTN_Q_TILE = 1024
KEY_CHUNK = 256
ONES_ROWS = 16
Q_STREAM = 128
Q_SCALE = COMP ** -0.5 * math.log2(math.e)

BF16 = jnp.bfloat16
F32 = jnp.float32


def _const_spec(shape):
    zeros = (0,) * len(shape)
    return pl.BlockSpec(shape, lambda *_: zeros, pipeline_mode=pl.Buffered(1))


def _rms(x):
    return x * lax.rsqrt(jnp.mean(x * x, axis=-1, keepdims=True) + EPS)


def _dot(a, b):
    return jnp.dot(a, b, preferred_element_type=F32)


def _ada_kernel(c_ref, w_ref, b_ref, o_ref):
    s = jax.nn.silu(c_ref[...]).astype(BF16)
    o_ref[0] = _dot(s, w_ref[0].astype(BF16)) + b_ref[0]


def _ada(cc, ada_w, ada_b):
    rows = cc.shape[0]
    tn = 1024
    return pl.pallas_call(
        _ada_kernel,
        grid=(DEPTH, N_MOD * D // tn),
        in_specs=[
            pl.BlockSpec((rows, D), lambda l, j: (0, 0)),
            pl.BlockSpec((1, D, tn), lambda l, j: (l, 0, j)),
            pl.BlockSpec((1, 1, tn), lambda l, j: (l, 0, j)),
        ],
        out_specs=pl.BlockSpec((1, rows, tn), lambda l, j: (l, 0, j)),
        out_shape=jax.ShapeDtypeStruct((DEPTH, rows, N_MOD * D), F32),
        name="ada",
    )(cc, ada_w, ada_b.reshape(DEPTH, 1, N_MOD * D))


def _ffn_kernel(*refs, n_mix, mix_bias, mod_idx, emit_idx, final_norm):
    refs = list(refs)
    x_ref, mod_ref = refs[:2]
    pos = 2
    mix_refs = refs[pos:pos + 2 * n_mix]
    pos += 2 * n_mix
    bmix_ref = None
    if mix_bias:
        bmix_ref = refs[pos]
        pos += 1
    wgu_ref, wd_ref = refs[pos:pos + 2]
    pos += 2
    fng_ref = None
    if final_norm:
        fng_ref = refs[pos]
        pos += 1
    out_ref = refs[pos]
    pos += 1
    hn_ref = None
    if emit_idx is not None:
        hn_ref = refs[pos]

    def mod(i):
        return mod_ref[0, i:i + 1, :]

    x = x_ref[0]
    if n_mix:
        y = None
        for m in range(n_mix):
            t = _dot(mix_refs[2 * m][0].astype(BF16), mix_refs[2 * m + 1][...])
            y = t if y is None else y + t
        if mix_bias:
            y = y + bmix_ref[...]
        x = x + mod(5) * y

    sh, sc, gt = mod_idx
    h = (_rms(x) * (1.0 + mod(sc)) + mod(sh)).astype(BF16)
    acc = None
    for f in range(D_FF // FF_CHUNK):
        cols = slice(f * FF_CHUNK, (f + 1) * FF_CHUNK)
        up_cols = slice(D_FF + f * FF_CHUNK, D_FF + (f + 1) * FF_CHUNK)
        g = _dot(h, wgu_ref[0, 0, :, cols])
        u = _dot(h, wgu_ref[0, 0, :, up_cols])
        a = (jax.nn.silu(g) * u).astype(BF16)
        t = _dot(a, wd_ref[0, 0, cols, :])
        acc = t if acc is None else acc + t

    out = x + (0.5 * mod(gt)) * acc
    if emit_idx is not None:
        esh, esc = emit_idx
        hn_ref[0] = (_rms(out) * (1.0 + mod(esc)) + mod(esh)).astype(hn_ref.dtype)
    if final_norm:
        out = _rms(out) * fng_ref[...]
    out_ref[0] = out


def _ffn(x, mod_l, mod_row, wgu, wd, w_idx, *, mod_idx, mixes=(), mix_bias=None,
         emit_idx=None, emit_dtype=BF16, final_g=None):
    nb, nt, _ = x.shape
    tm = min(TOKEN_TILE, nt)
    tok = lambda b, t: (b, t, 0)
    in_specs = [
        pl.BlockSpec((1, tm, D), tok),
        pl.BlockSpec((1, N_MOD, D), lambda b, t: (mod_row(b), 0, 0)),
    ]
    args = [x, mod_l]
    for arr, w in mixes:
        in_specs += [pl.BlockSpec((1, tm, arr.shape[-1]), tok), _const_spec(w.shape)]
        args += [arr, w]
    if mix_bias is not None:
        in_specs.append(_const_spec((1, D)))
        args.append(mix_bias.reshape(1, D))
    for w in (wgu, wd):
        in_specs.append(pl.BlockSpec((1, 1) + w.shape[2:], lambda b, t: w_idx + (0, 0),
                                     pipeline_mode=pl.Buffered(1)))
    args += [wgu, wd]
    if final_g is not None:
        in_specs.append(_const_spec((1, D)))
        args.append(final_g.reshape(1, D))
    out_specs = [pl.BlockSpec((1, tm, D), tok)]
    out_shape = [jax.ShapeDtypeStruct(x.shape, F32)]
    if emit_idx is not None:
        out_specs.append(pl.BlockSpec((1, tm, D), tok))
        out_shape.append(jax.ShapeDtypeStruct(x.shape, emit_dtype))
    kern = functools.partial(
        _ffn_kernel, n_mix=len(mixes), mix_bias=mix_bias is not None, mod_idx=mod_idx,
        emit_idx=emit_idx, final_norm=final_g is not None)
    res = pl.pallas_call(
        kern,
        grid=(nb, nt // tm),
        in_specs=in_specs,
        out_specs=out_specs,
        out_shape=out_shape,
        compiler_params=pltpu.CompilerParams(
            dimension_semantics=("parallel", "parallel"), vmem_limit_bytes=VMEM_LIMIT),
        name="ffn",
    )(*args)
    return res if emit_idx is not None else res[0]


def _inproj_kernel(h_ref, w_ref, cos_ref, sa_ref, sb_ref, ng_ref, nb_ref, ws_ref, bs_ref,
                   q_ref, k_ref, v_ref, g_ref):
    h = h_ref[0]
    tm = h.shape[0]

    def rope(x, scale):
        up = pltpu.roll(x, HEAD_W - ROPE_FREQS, 1)
        dn = pltpu.roll(x, ROPE_FREQS, 1)
        r = x * cos_ref[...] + up * sa_ref[...] + dn * sb_ref[...]
        return r * scale if scale != 1.0 else r

    for ref, col0, scale in ((q_ref, 0, Q_SCALE), (k_ref, QKV_W, 1.0)):
        p = _dot(h, w_ref[:, col0:col0 + QKV_W])
        for hd in range(HEADS):
            sl = slice(hd * HEAD_W, (hd + 1) * HEAD_W)
            ref[0, :, sl] = rope(p[:, sl], scale).astype(BF16)
    v_ref[0] = _dot(h, w_ref[:, 2 * QKV_W:3 * QKV_W]).T.astype(BF16)

    z = jax.nn.gelu(_dot(h, w_ref[:, 3 * QKV_W:]), approximate=True)
    u = z[:, :GMLP_W]
    vg = z[:, GMLP_W:]
    for grp in range(GMLP_W // GMLP_CHUNK):
        sl = slice(grp * GMLP_CHUNK, (grp + 1) * GMLP_CHUNK)
        t = vg[:, sl]
        mu = jnp.mean(t, axis=-1, keepdims=True)
        tc = t - mu
        var = jnp.mean(tc * tc, axis=-1, keepdims=True)
        tn = (tc * lax.rsqrt(var + EPS) * ng_ref[:, sl] + nb_ref[:, sl]).astype(BF16)
        for ch in range(tm // GMLP_CHUNK):
            rows = slice(ch * GMLP_CHUNK, (ch + 1) * GMLP_CHUNK)
            mixed = _dot(ws_ref[grp], tn[rows]) + bs_ref[:, sl]
            g_ref[0, rows, sl] = (u[rows, sl] * mixed).astype(BF16)


def _inproj(h, w_in, cos_t, sa_t, sb_t, norm_g, norm_b, w_s, b_s):
    nb, nt, _ = h.shape
    tm = TOKEN_TILE
    tok = lambda b, t: (b, t, 0)
    tab = pl.BlockSpec((tm, HEAD_W), lambda b, t: (t, 0))
    out = jax.ShapeDtypeStruct((nb, nt, QKV_W), BF16)
    out_t = jax.ShapeDtypeStruct((nb, QKV_W, nt), BF16)
    tok_spec = pl.BlockSpec((1, tm, QKV_W), tok)
    return pl.pallas_call(
        _inproj_kernel,
        grid=(nb, nt // tm),
        in_specs=[pl.BlockSpec((1, tm, D), tok), _const_spec(w_in.shape), tab, tab, tab,
                  _const_spec(norm_g.shape), _const_spec(norm_b.shape),
                  _const_spec(w_s.shape), _const_spec(b_s.shape)],
        out_specs=[tok_spec, tok_spec, pl.BlockSpec((1, QKV_W, tm), lambda b, t: (b, 0, t)), tok_spec],
        out_shape=[out, out, out_t, out],
        compiler_params=pltpu.CompilerParams(
            dimension_semantics=("parallel", "parallel"), vmem_limit_bytes=VMEM_LIMIT),
        name="inproj",
    )(h, w_in, cos_t, sa_t, sb_t, norm_g, norm_b, w_s, b_s)


def _ctx_kv_kernel(h_ref, w_ref, k_ref, v_ref):
    h = h_ref[0]
    k_ref[0] = _dot(h, w_ref[:, QKV_W:2 * QKV_W]).astype(BF16)
    v_ref[0] = _dot(h, w_ref[:, 2 * QKV_W:3 * QKV_W]).T.astype(BF16)


def _ctx_kv(h, w_in):
    nb, nt, _ = h.shape
    blk = lambda b: (b, 0, 0)
    return pl.pallas_call(
        _ctx_kv_kernel,
        grid=(nb,),
        in_specs=[pl.BlockSpec((1, nt, D), blk), _const_spec(w_in.shape)],
        out_specs=[pl.BlockSpec((1, nt, QKV_W), blk), pl.BlockSpec((1, QKV_W, nt), blk)],
        out_shape=[jax.ShapeDtypeStruct((nb, nt, QKV_W), BF16), jax.ShapeDtypeStruct((nb, QKV_W, nt), BF16)],
        compiler_params=pltpu.CompilerParams(vmem_limit_bytes=VMEM_LIMIT),
        name="ctx_kv",
    )(h, w_in)


def _rope_tables(seq):
    rows = seq // GRID_W
    row = jnp.repeat(jnp.arange(rows), GRID_W)
    col = jnp.tile(jnp.arange(GRID_W), rows)
    inv = ROPE_BASE ** (-jnp.arange(ROPE_FREQS, dtype=F32) / ROPE_FREQS)
    ang = jnp.stack([row[:, None] * inv, col[:, None] * inv], axis=1)
    cos, sin = jnp.cos(ang), jnp.sin(ang)
    zero = jnp.zeros_like(sin)
    cos_c = jnp.stack([cos, cos], axis=2).reshape(seq, COMP)
    sa_c = jnp.stack([-sin, zero], axis=2).reshape(seq, COMP)
    sb_c = jnp.stack([zero, sin], axis=2).reshape(seq, COMP)
    two = lambda t: jnp.concatenate([t, t], axis=-1)
    return two(cos_c), two(sa_c), two(sb_c)


def _attn_kernel(q_ref, k_ref, vt_ref, kc_ref, vct_ref, lam_ref, g_ref, o_ref, *, lam_init):
    lv = lam_ref[...]
    lam = (jnp.exp(jnp.sum(lv[0:1] * lv[1:2], axis=-1, keepdims=True))
           - jnp.exp(jnp.sum(lv[2:3] * lv[3:4], axis=-1, keepdims=True)) + lam_init)
    tq = q_ref.shape[1]
    sq = Q_STREAM
    nt_dims = (((1,), (1,)), ((), ()))
    chunks = [(kr, vr, slice(k0, min(k0 + KEY_CHUNK, kr.shape[1])))
              for kr, vr in ((k_ref, vt_ref), (kc_ref, vct_ref)) for k0 in range(0, kr.shape[1], KEY_CHUNK)]
    lane = lax.broadcasted_iota(jnp.int32, (sq, HEAD_W), 1)

    qq = []
    for j in range(tq // sq):
        qh = q_ref[0, j * sq:(j + 1) * sq, :]
        zero = jnp.zeros_like(qh)
        qq.append(jnp.concatenate([jnp.where(lane < COMP, qh, zero),
                                   jnp.where(lane >= COMP, qh, zero)], axis=0))

    def scores(chunk, j):
        kr, _, rows = chunk
        return lax.dot_general(kr[0, rows, :], qq[j], nt_dims, preferred_element_type=F32)

    n_streams = tq // sq
    m = [jnp.full((1, 2 * sq), -1e30, F32)] * n_streams
    acc = [jnp.zeros((HEAD_W + ONES_ROWS, 2 * sq), F32)] * n_streams
    s_cur = [scores(chunks[0], j) for j in range(n_streams)]
    for c, (_, vr, rows) in enumerate(chunks):
        ones = jnp.ones((ONES_ROWS, rows.stop - rows.start), BF16)
        vt = jnp.concatenate([vr[0, :, rows], ones], axis=0)
        for j in range(n_streams):
            s = s_cur[j]
            if c + 1 < len(chunks):
                s_cur[j] = scores(chunks[c + 1], j)
            m_new = jnp.maximum(m[j], jnp.max(s, axis=0, keepdims=True))
            e = jnp.exp2((s - m_new).astype(BF16))
            acc[j] = acc[j] * jnp.exp2(m[j] - m_new) + _dot(vt, e)
            m[j] = m_new
    for j in range(n_streams):
        r = 1.0 / acc[j][HEAD_W:HEAD_W + 1]
        o_t = acc[j][:HEAD_W, :sq] * r[:, :sq] - acc[j][:HEAD_W, sq:] * (lam * r[:, sq:])
        y = _rms(o_t.T) * (g_ref[...] * (1.0 - lam_init))
        o_ref[0, j * sq:(j + 1) * sq, :] = y.astype(BF16)


def _attn(q, k, vt, kc, vct, lam_vec, subln_g, lam_init):
    nb, nt, _ = q.shape
    nc = kc.shape[1]
    tq = ATTN_Q_TILE
    assert nt % KEY_CHUNK == 0 and nt % tq == 0
    keys = lambda n: pl.BlockSpec((1, n, HEAD_W), lambda b, h, t: (b, 0, h))
    vals = lambda n: pl.BlockSpec((1, HEAD_W, n), lambda b, h, t: (b, h, 0))
    return pl.pallas_call(
        functools.partial(_attn_kernel, lam_init=lam_init),
        grid=(nb, HEADS, nt // tq),
        in_specs=[pl.BlockSpec((1, tq, HEAD_W), lambda b, h, t: (b, t, h)),
                  keys(nt), vals(nt), keys(nc), vals(nc),
                  _const_spec(lam_vec.shape), _const_spec((1, HEAD_W))],
        out_specs=pl.BlockSpec((1, tq, HEAD_W), lambda b, h, t: (b, t, h)),
        out_shape=jax.ShapeDtypeStruct((nb, nt, QKV_W), BF16),
        compiler_params=pltpu.CompilerParams(
            dimension_semantics=("parallel", "parallel", "parallel"), vmem_limit_bytes=VMEM_LIMIT),
        name="attn",
    )(q, k, vt, kc, vct, lam_vec, subln_g.reshape(1, HEAD_W))


def _fourier_consts():
    r, s, gc = DFT_R, SUB, FOURIER_GROUP_CH
    idx = np.arange(r)
    a64 = 2.0 * np.pi * np.outer(idx, idx) / r
    c64, s64 = np.cos(a64), np.sin(a64)
    eye = np.eye(s)
    ach = 2.0 * np.pi * np.outer(np.arange(gc), np.arange(gc)) / gc
    scale = 1.0 / math.sqrt(r * r * gc)
    cd = np.concatenate([np.cos(ach), -np.sin(ach)], axis=1) * scale
    q = np.block([[np.kron(c64, eye), np.kron(s64, eye)],
                  [np.kron(-s64, eye), np.kron(c64, eye)]])
    n2 = np.arange(r).reshape(s, 1, s)
    k1 = idx.reshape(1, r, 1)
    at = 2.0 * np.pi * (n2 * k1) / (r * r)
    tw = lambda t: np.broadcast_to(t.reshape(s, r * s, 1), (s, r * s, 128))
    pc = np.einsum("kgn,ab->kagbn", c64.reshape(r, s, s), eye).reshape(r * s, r * s)
    ps = np.einsum("kgn,ab->kagbn", s64.reshape(r, s, s), eye).reshape(r * s, r * s)
    p = np.concatenate([pc, ps], axis=1)
    bf = lambda t: jnp.asarray(t, dtype=F32).astype(BF16)
    return (bf(cd), bf(q), jnp.asarray(tw(np.cos(at)), F32), jnp.asarray(tw(np.sin(at)), F32), bf(p))


def _fourier_kernel(h_ref, cd_ref, q_ref, tc_ref, ts_ref, p_ref, o_ref, t_scr):
    r, s, gc = DFT_R, SUB, FOURIER_GROUP_CH
    rows = r * s
    wide = lambda t: jnp.concatenate([t, t], axis=1)
    for g in range(s):
        hg = h_ref[0, :, g, :, :].reshape(rows, gc).astype(BF16)
        z = _dot(hg, cd_ref[...])
        zs = jnp.concatenate([z[:, :gc], z[:, gc:]], axis=0).astype(BF16)
        y = _dot(q_ref[...], zs)
        yr, yi = y[:rows], y[rows:]
        tc, ts = wide(tc_ref[g]), wide(ts_ref[g])
        t_scr[0, :, g] = (yr * tc + yi * ts).astype(BF16).reshape(s, r, gc)
        t_scr[1, :, g] = (yi * tc - yr * ts).astype(BF16).reshape(s, r, gc)
    for kg in range(s):
        t = jnp.concatenate([t_scr[0, kg].reshape(rows, gc), t_scr[1, kg].reshape(rows, gc)], axis=0)
        o_ref[0, :, kg, :, :] = _dot(p_ref[...], t).reshape(r, s, gc)


def _fourier(h, consts):
    nb, seq, _ = h.shape
    r, s, gc = DFT_R, SUB, FOURIER_GROUP_CH
    cd, q, tc, ts, p = consts
    h5 = h.reshape(nb, r, s, s, D)
    blk = pl.BlockSpec((1, r, s, s, gc), lambda b, c: (b, 0, 0, 0, c))
    out = pl.pallas_call(
        _fourier_kernel,
        grid=(nb, D // gc),
        in_specs=[blk, _const_spec(cd.shape), _const_spec(q.shape), _const_spec(tc.shape),
                  _const_spec(ts.shape), _const_spec(p.shape)],
        out_specs=blk,
        out_shape=jax.ShapeDtypeStruct(h5.shape, F32),
        scratch_shapes=[pltpu.VMEM((2, s, s, r, gc), BF16)],
        compiler_params=pltpu.CompilerParams(
            dimension_semantics=("parallel", "parallel"), vmem_limit_bytes=VMEM_LIMIT),
        name="fourier",
    )(h5, cd, q, tc, ts, p)
    return out.reshape(nb, seq, D)


def kernel(x, c, ctx, c_ctx, ada_w, ada_b, ffn_w_gu, ffn_w_down, mix_w_in, mix_w_out, diff_lambda,
           diff_subln_g, sgu_norm_g, sgu_norm_b, sgu_w, sgu_b, fourier_w_out, fourier_b_out, final_norm_g):
    nb, seq, _ = x.shape
    assert seq == DFT_R * DFT_R and DEPTH == 2
    ctx_row = nb
    cc = jnp.concatenate([c, c_ctx[None, :], jnp.zeros((2 * SUB - nb - 1, D), F32)], axis=0)
    mods = _ada(cc, ada_w, ada_b).reshape(DEPTH, cc.shape[0], N_MOD, D)
    latent = lambda b: b
    shared = lambda b: ctx_row

    ffn_w = (ffn_w_gu.astype(BF16), ffn_w_down.astype(BF16))

    x, h = _ffn(x, mods[0], latent, *ffn_w, (0, 0), mod_idx=(0, 1, 2), emit_idx=(3, 4))
    _, hc = _ffn(ctx.reshape(1, -1, D), mods[0], shared, *ffn_w, (0, 0), mod_idx=(0, 1, 2), emit_idx=(3, 4))
    hc = hc.reshape(ctx.shape)

    w_in = mix_w_in[0].astype(BF16)
    cos_t, sa_t, sb_t = _rope_tables(seq)
    bs_rows = jnp.repeat(sgu_b[0].T, GMLP_CHUNK, axis=1)
    q, k, vt, gated = _inproj(h, w_in, cos_t, sa_t, sb_t,
                             sgu_norm_g[0].reshape(1, GMLP_W), sgu_norm_b[0].reshape(1, GMLP_W),
                             sgu_w[0].astype(BF16), bs_rows)
    kc, vct = _ctx_kv(hc, w_in)
    lam_init = 0.8 - 0.6 * math.exp(-0.3 * 0)
    att = _attn(q, k, vt, kc, vct,
                diff_lambda[0], diff_subln_g[0], lam_init)
    w_out = mix_w_out[0].astype(BF16)
    x = _ffn(x, mods[0], latent, *ffn_w, (0, 1), mod_idx=(6, 7, 8),
             mixes=((att, w_out[:QKV_W]), (gated, w_out[QKV_W:])))

    x, h = _ffn(x, mods[1], latent, *ffn_w, (1, 0), mod_idx=(0, 1, 2), emit_idx=(3, 4), emit_dtype=F32)
    f = _fourier(h, _fourier_consts())
    return _ffn(x, mods[1], latent, *ffn_w, (1, 1), mod_idx=(6, 7, 8),
                mixes=((f, fourier_w_out[0].astype(BF16)),), mix_bias=fourier_b_out[0],
                final_g=final_norm_g)
```

```python
import functools
import math

import numpy as np
import jax
import jax.numpy as jnp
from jax import lax
from jax.experimental import pallas as pl
from jax.experimental.pallas import tpu as pltpu

D = 1024
DEPTH = 2
N_MOD = 9
D_FF = 2816
EPS = 1e-6
GRID_W = 64
HEADS = 4
HEAD_W = 128
COMP = 64
ROPE_FREQS = 16
ROPE_BASE = 10000.0
QKV_W = HEADS * HEAD_W
GMLP_W = 512
GMLP_CHUNK = 128
IN_COLS = 3 * QKV_W + 2 * GMLP_W
FOURIER_GROUP_CH = 256
DFT_R = 64
SUB = 8

VMEM_LIMIT = 56 * 1024 * 1024
FF_CHUNK = 256
TOKEN_TILE = 512
INPROJ_TILE = 1024
ATTN_Q_TILE = 1024
KEY_CHUNK = 256
ONES_ROWS = 16
Q_STREAM = 128
Q_SCALE = COMP ** -0.5 * math.log2(math.e)

BF16 = jnp.bfloat16
F32 = jnp.float32


def _const_spec(shape):
    zeros = (0,) * len(shape)
    return pl.BlockSpec(shape, lambda *_: zeros, pipeline_mode=pl.Buffered(1))


def _rms(x):
    return x * lax.rsqrt(jnp.mean(x * x, axis=-1, keepdims=True) + EPS)


def _dot(a, b):
    return jnp.dot(a, b, preferred_element_type=F32)


def _ada_kernel(c_ref, w_ref, b_ref, o_ref):
    s = jax.nn.silu(c_ref[...]).astype(BF16)
    o_ref[0] = _dot(s, w_ref[0].astype(BF16)) + b_ref[0]


def _ada(cc, ada_w, ada_b):
    rows = cc.shape[0]
    tn = 1024
    return pl.pallas_call(
        _ada_kernel,
        grid=(DEPTH, N_MOD * D // tn),
        in_specs=[
            pl.BlockSpec((rows, D), lambda l, j: (0, 0)),
            pl.BlockSpec((1, D, tn), lambda l, j: (l, 0, j)),
            pl.BlockSpec((1, 1, tn), lambda l, j: (l, 0, j)),
        ],
        out_specs=pl.BlockSpec((1, rows, tn), lambda l, j: (l, 0, j)),
        out_shape=jax.ShapeDtypeStruct((DEPTH, rows, N_MOD * D), F32),
        name="ada",
    )(cc, ada_w, ada_b.reshape(DEPTH, 1, N_MOD * D))


def _ffn_kernel(*refs, n_mix, mix_bias, mod_idx, emit_idx, final_norm):
    refs = list(refs)
    x_ref, mod_ref = refs[:2]
    pos = 2
    mix_refs = refs[pos:pos + 2 * n_mix]
    pos += 2 * n_mix
    bmix_ref = None
    if mix_bias:
        bmix_ref = refs[pos]
        pos += 1
    wgu_ref, wd_ref = refs[pos:pos + 2]
    pos += 2
    fng_ref = None
    if final_norm:
        fng_ref = refs[pos]
        pos += 1
    out_ref = refs[pos]
    pos += 1
    hn_ref = None
    if emit_idx is not None:
        hn_ref = refs[pos]

    def mod(i):
        return mod_ref[0, i:i + 1, :]

    x = x_ref[0]
    if n_mix:
        y = None
        for m in range(n_mix):
            t = _dot(mix_refs[2 * m][0].astype(BF16), mix_refs[2 * m + 1][...])
            y = t if y is None else y + t
        if mix_bias:
            y = y + bmix_ref[...]
        x = x + mod(5) * y

    sh, sc, gt = mod_idx
    h = (_rms(x) * (1.0 + mod(sc)) + mod(sh)).astype(BF16)
    acc = None
    for f in range(D_FF // FF_CHUNK):
        cols = slice(f * FF_CHUNK, (f + 1) * FF_CHUNK)
        up_cols = slice(D_FF + f * FF_CHUNK, D_FF + (f + 1) * FF_CHUNK)
        g = _dot(h, wgu_ref[0, 0, :, cols])
        u = _dot(h, wgu_ref[0, 0, :, up_cols])
        a = (jax.nn.silu(g) * u).astype(BF16)
        t = _dot(a, wd_ref[0, 0, cols, :])
        acc = t if acc is None else acc + t

    out = x + (0.5 * mod(gt)) * acc
    if emit_idx is not None:
        esh, esc = emit_idx
        hn_ref[0] = (_rms(out) * (1.0 + mod(esc)) + mod(esh)).astype(hn_ref.dtype)
    if final_norm:
        out = _rms(out) * fng_ref[...]
    out_ref[0] = out


def _ffn(x, mod_l, mod_row, wgu, wd, w_idx, *, mod_idx, mixes=(), mix_bias=None,
         emit_idx=None, emit_dtype=BF16, final_g=None):
    nb, nt, _ = x.shape
    tm = min(TOKEN_TILE, nt)
    tok = lambda b, t: (b, t, 0)
    in_specs = [
        pl.BlockSpec((1, tm, D), tok),
        pl.BlockSpec((1, N_MOD, D), lambda b, t: (mod_row(b), 0, 0)),
    ]
    args = [x, mod_l]
    for arr, w in mixes:
        in_specs += [pl.BlockSpec((1, tm, arr.shape[-1]), tok), _const_spec(w.shape)]
        args += [arr, w]
    if mix_bias is not None:
        in_specs.append(_const_spec((1, D)))
        args.append(mix_bias.reshape(1, D))
    for w in (wgu, wd):
        in_specs.append(pl.BlockSpec((1, 1) + w.shape[2:], lambda b, t: w_idx + (0, 0),
                                     pipeline_mode=pl.Buffered(1)))
    args += [wgu, wd]
    if final_g is not None:
        in_specs.append(_const_spec((1, D)))
        args.append(final_g.reshape(1, D))
    out_specs = [pl.BlockSpec((1, tm, D), tok)]
    out_shape = [jax.ShapeDtypeStruct(x.shape, F32)]
    if emit_idx is not None:
        out_specs.append(pl.BlockSpec((1, tm, D), tok))
        out_shape.append(jax.ShapeDtypeStruct(x.shape, emit_dtype))
    kern = functools.partial(
        _ffn_kernel, n_mix=len(mixes), mix_bias=mix_bias is not None, mod_idx=mod_idx,
        emit_idx=emit_idx, final_norm=final_g is not None)
    res = pl.pallas_call(
        kern,
        grid=(nb, nt // tm),
        in_specs=in_specs,
        out_specs=out_specs,
        out_shape=out_shape,
        compiler_params=pltpu.CompilerParams(
            dimension_semantics=("parallel", "parallel"), vmem_limit_bytes=VMEM_LIMIT),
        name="ffn",
    )(*args)
    return res if emit_idx is not None else res[0]


def _inproj_kernel(h_ref, w_ref, cos_ref, sa_ref, sb_ref, ng_ref, nb_ref, ws_ref, bs_ref,
                   q_ref, k_ref, v_ref, g_ref):
    h = h_ref[0]
    tm = h.shape[0]

    def rope(x, scale):
        up = pltpu.roll(x, HEAD_W - ROPE_FREQS, 1)
        dn = pltpu.roll(x, ROPE_FREQS, 1)
        r = x * cos_ref[...] + up * sa_ref[...] + dn * sb_ref[...]
        return r * scale if scale != 1.0 else r

    def project_rope(ref, col0, scale):
        p = _dot(h, w_ref[:, col0:col0 + QKV_W])
        for hd in range(HEADS):
            sl = slice(hd * HEAD_W, (hd + 1) * HEAD_W)
            ref[0, :, sl] = rope(p[:, sl], scale).astype(BF16)

    z0 = 3 * QKV_W
    vg = jax.nn.gelu(_dot(h, w_ref[:, z0 + GMLP_W:]), approximate=True)
    u = jax.nn.gelu(_dot(h, w_ref[:, z0:z0 + GMLP_W]), approximate=True)
    tn = []
    for grp in range(GMLP_W // GMLP_CHUNK):
        sl = slice(grp * GMLP_CHUNK, (grp + 1) * GMLP_CHUNK)
        t = vg[:, sl]
        mu = jnp.mean(t, axis=-1, keepdims=True)
        tc = t - mu
        var = jnp.mean(tc * tc, axis=-1, keepdims=True)
        tn.append((tc * lax.rsqrt(var + EPS) * ng_ref[:, sl] + nb_ref[:, sl]).astype(BF16))
    project_rope(q_ref, 0, Q_SCALE)
    for grp in range(GMLP_W // GMLP_CHUNK):
        sl = slice(grp * GMLP_CHUNK, (grp + 1) * GMLP_CHUNK)
        for ch in range(tm // GMLP_CHUNK):
            rows = slice(ch * GMLP_CHUNK, (ch + 1) * GMLP_CHUNK)
            mixed = _dot(ws_ref[grp], tn[grp][rows]) + bs_ref[:, sl]
            g_ref[0, rows, sl] = (u[rows, sl] * mixed).astype(BF16)
    project_rope(k_ref, QKV_W, 1.0)
    v_ref[0] = _dot(h, w_ref[:, 2 * QKV_W:3 * QKV_W]).T.astype(BF16)


def _inproj(h, w_in, cos_t, sa_t, sb_t, norm_g, norm_b, w_s, b_s):
    nb, nt, _ = h.shape
    tm = INPROJ_TILE
    tok = lambda b, t: (b, t, 0)
    tab = pl.BlockSpec((tm, HEAD_W), lambda b, t: (t, 0))
    out = jax.ShapeDtypeStruct((nb, nt, QKV_W), BF16)
    out_t = jax.ShapeDtypeStruct((nb, QKV_W, nt), BF16)
    tok_spec = pl.BlockSpec((1, tm, QKV_W), tok)
    return pl.pallas_call(
        _inproj_kernel,
        grid=(nb, nt // tm),
        in_specs=[pl.BlockSpec((1, tm, D), tok), _const_spec(w_in.shape), tab, tab, tab,
                  _const_spec(norm_g.shape), _const_spec(norm_b.shape),
                  _const_spec(w_s.shape), _const_spec(b_s.shape)],
        out_specs=[tok_spec, tok_spec, pl.BlockSpec((1, QKV_W, tm), lambda b, t: (b, 0, t)), tok_spec],
        out_shape=[out, out, out_t, out],
        compiler_params=pltpu.CompilerParams(
            dimension_semantics=("parallel", "parallel"), vmem_limit_bytes=VMEM_LIMIT),
        name="inproj",
    )(h, w_in, cos_t, sa_t, sb_t, norm_g, norm_b, w_s, b_s)


def _ctx_kv_kernel(h_ref, w_ref, k_ref, v_ref):
    h = h_ref[0]
    k_ref[0] = _dot(h, w_ref[:, QKV_W:2 * QKV_W]).astype(BF16)
    v_ref[0] = _dot(h, w_ref[:, 2 * QKV_W:3 * QKV_W]).T.astype(BF16)


def _ctx_kv(h, w_in):
    nb, nt, _ = h.shape
    blk = lambda b: (b, 0, 0)
    return pl.pallas_call(
        _ctx_kv_kernel,
        grid=(nb,),
        in_specs=[pl.BlockSpec((1, nt, D), blk), _const_spec(w_in.shape)],
        out_specs=[pl.BlockSpec((1, nt, QKV_W), blk), pl.BlockSpec((1, QKV_W, nt), blk)],
        out_shape=[jax.ShapeDtypeStruct((nb, nt, QKV_W), BF16), jax.ShapeDtypeStruct((nb, QKV_W, nt), BF16)],
        compiler_params=pltpu.CompilerParams(vmem_limit_bytes=VMEM_LIMIT),
        name="ctx_kv",
    )(h, w_in)


def _rope_tables(seq):
    rows = seq // GRID_W
    row = jnp.repeat(jnp.arange(rows), GRID_W)
    col = jnp.tile(jnp.arange(GRID_W), rows)
    inv = ROPE_BASE ** (-jnp.arange(ROPE_FREQS, dtype=F32) / ROPE_FREQS)
    ang = jnp.stack([row[:, None] * inv, col[:, None] * inv], axis=1)
    cos, sin = jnp.cos(ang), jnp.sin(ang)
    zero = jnp.zeros_like(sin)
    cos_c = jnp.stack([cos, cos], axis=2).reshape(seq, COMP)
    sa_c = jnp.stack([-sin, zero], axis=2).reshape(seq, COMP)
    sb_c = jnp.stack([zero, sin], axis=2).reshape(seq, COMP)
    two = lambda t: jnp.concatenate([t, t], axis=-1)
    return two(cos_c), two(sa_c), two(sb_c)


def _attn_kernel(q_ref, k_ref, vt_ref, kc_ref, vct_ref, lam_ref, g_ref, o_ref, *, lam_init):
    lv = lam_ref[...]
    lam = (jnp.exp(jnp.sum(lv[0:1] * lv[1:2], axis=-1, keepdims=True))
           - jnp.exp(jnp.sum(lv[2:3] * lv[3:4], axis=-1, keepdims=True)) + lam_init)
    tq = q_ref.shape[1]
    sq = Q_STREAM
    nt_dims = (((1,), (1,)), ((), ()))
    chunks = [(kr, vr, slice(k0, min(k0 + KEY_CHUNK, kr.shape[1])))
              for kr, vr in ((k_ref, vt_ref), (kc_ref, vct_ref)) for k0 in range(0, kr.shape[1], KEY_CHUNK)]
    lane = lax.broadcasted_iota(jnp.int32, (sq, HEAD_W), 1)

    qq = []
    for j in range(tq // sq):
        qh = q_ref[0, j * sq:(j + 1) * sq, :]
        zero = jnp.zeros_like(qh)
        qq.append(jnp.concatenate([jnp.where(lane < COMP, qh, zero),
                                   jnp.where(lane >= COMP, qh, zero)], axis=0))

    def scores(chunk, j):
        kr, _, rows = chunk
        return lax.dot_general(kr[0, rows, :], qq[j], nt_dims, preferred_element_type=F32)

    n_streams = tq // sq
    m = [jnp.full((1, 2 * sq), -1e30, F32)] * n_streams
    acc = [jnp.zeros((HEAD_W + ONES_ROWS, 2 * sq), F32)] * n_streams
    s_cur = [scores(chunks[0], j) for j in range(n_streams)]
    for c, (_, vr, rows) in enumerate(chunks):
        ones = jnp.ones((ONES_ROWS, rows.stop - rows.start), BF16)
        vt = jnp.concatenate([vr[0, :, rows], ones], axis=0)
        for j in range(n_streams):
            s = s_cur[j]
            if c + 1 < len(chunks):
                s_cur[j] = scores(chunks[c + 1], j)
            m_new = jnp.maximum(m[j], jnp.max(s, axis=0, keepdims=True))
            e = jnp.exp2((s - m_new).astype(BF16))
            acc[j] = acc[j] * jnp.exp2(m[j] - m_new) + _dot(vt, e)
            m[j] = m_new
    for j in range(n_streams):
        r = 1.0 / acc[j][HEAD_W:HEAD_W + 1]
        o_t = acc[j][:HEAD_W, :sq] * r[:, :sq] - acc[j][:HEAD_W, sq:] * (lam * r[:, sq:])
        y = _rms(o_t.T) * (g_ref[...] * (1.0 - lam_init))
        o_ref[0, j * sq:(j + 1) * sq, :] = y.astype(BF16)


def _attn(q, k, vt, kc, vct, lam_vec, subln_g, lam_init):
    nb, nt, _ = q.shape
    nc = kc.shape[1]
    tq = ATTN_Q_TILE
    assert nt % KEY_CHUNK == 0 and nt % tq == 0
    keys = lambda n: pl.BlockSpec((1, n, HEAD_W), lambda b, h, t: (b, 0, h))
    vals = lambda n: pl.BlockSpec((1, HEAD_W, n), lambda b, h, t: (b, h, 0))
    return pl.pallas_call(
        functools.partial(_attn_kernel, lam_init=lam_init),
        grid=(nb, HEADS, nt // tq),
        in_specs=[pl.BlockSpec((1, tq, HEAD_W), lambda b, h, t: (b, t, h)),
                  keys(nt), vals(nt), keys(nc), vals(nc),
                  _const_spec(lam_vec.shape), _const_spec((1, HEAD_W))],
        out_specs=pl.BlockSpec((1, tq, HEAD_W), lambda b, h, t: (b, t, h)),
        out_shape=jax.ShapeDtypeStruct((nb, nt, QKV_W), BF16),
        compiler_params=pltpu.CompilerParams(
            dimension_semantics=("parallel", "parallel", "parallel"), vmem_limit_bytes=VMEM_LIMIT),
        name="attn",
    )(q, k, vt, kc, vct, lam_vec, subln_g.reshape(1, HEAD_W))


def _fourier_consts():
    r, s, gc = DFT_R, SUB, FOURIER_GROUP_CH
    idx = np.arange(r)
    a64 = 2.0 * np.pi * np.outer(idx, idx) / r
    c64, s64 = np.cos(a64), np.sin(a64)
    eye = np.eye(s)
    ach = 2.0 * np.pi * np.outer(np.arange(gc), np.arange(gc)) / gc
    scale = 1.0 / math.sqrt(r * r * gc)
    cd = np.concatenate([np.cos(ach), -np.sin(ach)], axis=1) * scale
    q = np.stack([np.kron(c64, eye), np.kron(s64 - c64, eye), np.kron(s64 + c64, eye)])
    n2 = np.arange(r).reshape(s, 1, s)
    k1 = idx.reshape(1, r, 1)
    at = 2.0 * np.pi * (n2 * k1) / (r * r)
    tw = lambda t: np.broadcast_to(t.reshape(s, r * s, 1), (s, r * s, 128))
    pc = np.einsum("kgn,ab->kagbn", c64.reshape(r, s, s), eye).reshape(r * s, r * s)
    ps = np.einsum("kgn,ab->kagbn", s64.reshape(r, s, s), eye).reshape(r * s, r * s)
    p = np.concatenate([pc, ps], axis=1)
    bf = lambda t: jnp.asarray(t, dtype=F32).astype(BF16)
    return (bf(cd), bf(q), jnp.asarray(tw(np.cos(at)), F32), jnp.asarray(tw(np.sin(at)), F32), bf(p))


def _fourier_kernel(h_ref, cd_ref, q_ref, tc_ref, ts_ref, p_ref, o_ref, t_scr):
    r, s, gc = DFT_R, SUB, FOURIER_GROUP_CH
    rows = r * s
    wide = lambda t: jnp.concatenate([t, t], axis=1)
    for g in range(s):
        hg = h_ref[0, :, g, :, :].reshape(rows, gc).astype(BF16)
        z = _dot(hg, cd_ref[...])
        zr, zi = z[:, :gc], z[:, gc:]
        t1 = _dot(q_ref[0], (zr + zi).astype(BF16))
        yr = t1 + _dot(q_ref[1], zi.astype(BF16))
        yi = t1 - _dot(q_ref[2], zr.astype(BF16))
        tc, ts = wide(tc_ref[g]), wide(ts_ref[g])
        t_scr[0, :, g] = (yr * tc + yi * ts).astype(BF16).reshape(s, r, gc)
        t_scr[1, :, g] = (yi * tc - yr * ts).astype(BF16).reshape(s, r, gc)
    for kg in range(s):
        t = jnp.concatenate([t_scr[0, kg].reshape(rows, gc), t_scr[1, kg].reshape(rows, gc)], axis=0)
        o_ref[0, :, kg, :, :] = _dot(p_ref[...], t).reshape(r, s, gc)


def _fourier(h, consts):
    nb, seq, _ = h.shape
    r, s, gc = DFT_R, SUB, FOURIER_GROUP_CH
    cd, q, tc, ts, p = consts
    h5 = h.reshape(nb, r, s, s, D)
    blk = pl.BlockSpec((1, r, s, s, gc), lambda b, c: (b, 0, 0, 0, c))
    out = pl.pallas_call(
        _fourier_kernel,
        grid=(nb, D // gc),
        in_specs=[blk, _const_spec(cd.shape), _const_spec(q.shape), _const_spec(tc.shape),
                  _const_spec(ts.shape), _const_spec(p.shape)],
        out_specs=blk,
        out_shape=jax.ShapeDtypeStruct(h5.shape, F32),
        scratch_shapes=[pltpu.VMEM((2, s, s, r, gc), BF16)],
        compiler_params=pltpu.CompilerParams(
            dimension_semantics=("parallel", "parallel"), vmem_limit_bytes=VMEM_LIMIT),
        name="fourier",
    )(h5, cd, q, tc, ts, p)
    return out.reshape(nb, seq, D)


def kernel(x, c, ctx, c_ctx, ada_w, ada_b, ffn_w_gu, ffn_w_down, mix_w_in, mix_w_out, diff_lambda,
           diff_subln_g, sgu_norm_g, sgu_norm_b, sgu_w, sgu_b, fourier_w_out, fourier_b_out, final_norm_g):
    nb, seq, _ = x.shape
    assert seq == DFT_R * DFT_R and DEPTH == 2
    ctx_row = nb
    cc = jnp.concatenate([c, c_ctx[None, :], jnp.zeros((2 * SUB - nb - 1, D), F32)], axis=0)
    mods = _ada(cc, ada_w, ada_b).reshape(DEPTH, cc.shape[0], N_MOD, D)
    latent = lambda b: b
    shared = lambda b: ctx_row

    ffn_w = (ffn_w_gu.astype(BF16), ffn_w_down.astype(BF16))

    x, h = _ffn(x, mods[0], latent, *ffn_w, (0, 0), mod_idx=(0, 1, 2), emit_idx=(3, 4))
    _, hc = _ffn(ctx.reshape(1, -1, D), mods[0], shared, *ffn_w, (0, 0), mod_idx=(0, 1, 2), emit_idx=(3, 4))
    hc = hc.reshape(ctx.shape)

    w_in = mix_w_in[0].astype(BF16)
    cos_t, sa_t, sb_t = _rope_tables(seq)
    bs_rows = jnp.repeat(sgu_b[0].T, GMLP_CHUNK, axis=1)
    q, k, vt, gated = _inproj(h, w_in, cos_t, sa_t, sb_t,
                             sgu_norm_g[0].reshape(1, GMLP_W), sgu_norm_b[0].reshape(1, GMLP_W),
                             sgu_w[0].astype(BF16), bs_rows)
    kc, vct = _ctx_kv(hc, w_in)
    lam_init = 0.8 - 0.6 * math.exp(-0.3 * 0)
    att = _attn(q, k, vt, kc, vct,
                diff_lambda[0], diff_subln_g[0], lam_init)
    w_out = mix_w_out[0].astype(BF16)
    x = _ffn(x, mods[0], latent, *ffn_w, (0, 1), mod_idx=(6, 7, 8),
             mixes=((att, w_out[:QKV_W]), (gated, w_out[QKV_W:])))

    x, h = _ffn(x, mods[1], latent, *ffn_w, (1, 0), mod_idx=(0, 1, 2), emit_idx=(3, 4), emit_dtype=F32)
    f = _fourier(h, _fourier_consts())
    return _ffn(x, mods[1], latent, *ffn_w, (1, 1), mod_idx=(6, 7, 8),
                mixes=((f, fourier_w_out[0].astype(BF16)),), mix_bias=fourier_b_out[0],
                final_g=final_norm_g)
```

```python
import functools
import math

import numpy as np
import jax
import jax.numpy as jnp
from jax import lax
from jax.experimental import pallas as pl
from jax.experimental.pallas import tpu as pltpu

D = 1024
DEPTH = 2
N_MOD = 9
D_FF = 2816
EPS = 1e-6
GRID_W = 64
HEADS = 4
HEAD_W = 128
COMP = 64
ROPE_FREQS = 16
ROPE_BASE = 10000.0
QKV_W = HEADS * HEAD_W
GMLP_W = 512
GMLP_CHUNK = 128
IN_COLS = 3 * QKV_W + 2 * GMLP_W
FOURIER_GROUP_CH = 256
DFT_R = 16
SUB = 8

VMEM_LIMIT = 56 * 1024 * 1024
FF_CHUNK = 256
TOKEN_TILE = 512
INPROJ_TILE = 1024
ATTN_Q_TILE = 1024
KEY_CHUNK = 256
ONES_ROWS = 16
Q_STREAM = 128
Q_SCALE = COMP ** -0.5 * math.log2(math.e)

BF16 = jnp.bfloat16
F32 = jnp.float32


def _const_spec(shape):
    zeros = (0,) * len(shape)
    return pl.BlockSpec(shape, lambda *_: zeros, pipeline_mode=pl.Buffered(1))


def _rms(x):
    return x * lax.rsqrt(jnp.mean(x * x, axis=-1, keepdims=True) + EPS)


def _dot(a, b):
    return jnp.dot(a, b, preferred_element_type=F32)


def _ada_kernel(c_ref, w_ref, b_ref, o_ref):
    s = jax.nn.silu(c_ref[...]).astype(BF16)
    o_ref[0] = _dot(s, w_ref[0].astype(BF16)) + b_ref[0]


def _ada(cc, ada_w, ada_b):
    rows = cc.shape[0]
    tn = 1024
    return pl.pallas_call(
        _ada_kernel,
        grid=(DEPTH, N_MOD * D // tn),
        in_specs=[
            pl.BlockSpec((rows, D), lambda l, j: (0, 0)),
            pl.BlockSpec((1, D, tn), lambda l, j: (l, 0, j)),
            pl.BlockSpec((1, 1, tn), lambda l, j: (l, 0, j)),
        ],
        out_specs=pl.BlockSpec((1, rows, tn), lambda l, j: (l, 0, j)),
        out_shape=jax.ShapeDtypeStruct((DEPTH, rows, N_MOD * D), F32),
        name="ada",
    )(cc, ada_w, ada_b.reshape(DEPTH, 1, N_MOD * D))


def _ffn_kernel(*refs, n_mix, mix_bias, mod_idx, emit_idx, final_norm):
    refs = list(refs)
    x_ref, mod_ref = refs[:2]
    pos = 2
    mix_refs = refs[pos:pos + 2 * n_mix]
    pos += 2 * n_mix
    bmix_ref = None
    if mix_bias:
        bmix_ref = refs[pos]
        pos += 1
    wgu_ref, wd_ref = refs[pos:pos + 2]
    pos += 2
    fng_ref = None
    if final_norm:
        fng_ref = refs[pos]
        pos += 1
    out_ref = refs[pos]
    pos += 1
    hn_ref = None
    if emit_idx is not None:
        hn_ref = refs[pos]

    def mod(i):
        return mod_ref[0, i:i + 1, :]

    x = x_ref[0]
    if n_mix:
        y = None
        for m in range(n_mix):
            t = _dot(mix_refs[2 * m][0].astype(BF16), mix_refs[2 * m + 1][...])
            y = t if y is None else y + t
        if mix_bias:
            y = y + bmix_ref[...]
        x = x + mod(5) * y

    sh, sc, gt = mod_idx
    h = (_rms(x) * (1.0 + mod(sc)) + mod(sh)).astype(BF16)
    acc = None
    for f in range(D_FF // FF_CHUNK):
        cols = slice(f * FF_CHUNK, (f + 1) * FF_CHUNK)
        up_cols = slice(D_FF + f * FF_CHUNK, D_FF + (f + 1) * FF_CHUNK)
        g = _dot(h, wgu_ref[0, 0, :, cols])
        u = _dot(h, wgu_ref[0, 0, :, up_cols])
        a = (jax.nn.silu(g) * u).astype(BF16)
        t = _dot(a, wd_ref[0, 0, cols, :])
        acc = t if acc is None else acc + t

    out = x + (0.5 * mod(gt)) * acc
    if emit_idx is not None:
        esh, esc = emit_idx
        hn_ref[0] = (_rms(out) * (1.0 + mod(esc)) + mod(esh)).astype(hn_ref.dtype)
    if final_norm:
        out = _rms(out) * fng_ref[...]
    out_ref[0] = out


def _ffn(x, mod_l, mod_row, wgu, wd, w_idx, *, mod_idx, mixes=(), mix_bias=None,
         emit_idx=None, emit_dtype=BF16, final_g=None):
    nb, nt, _ = x.shape
    tm = min(TOKEN_TILE, nt)
    tok = lambda b, t: (b, t, 0)
    in_specs = [
        pl.BlockSpec((1, tm, D), tok),
        pl.BlockSpec((1, N_MOD, D), lambda b, t: (mod_row(b), 0, 0)),
    ]
    args = [x, mod_l]
    for arr, w in mixes:
        in_specs += [pl.BlockSpec((1, tm, arr.shape[-1]), tok), _const_spec(w.shape)]
        args += [arr, w]
    if mix_bias is not None:
        in_specs.append(_const_spec((1, D)))
        args.append(mix_bias.reshape(1, D))
    for w in (wgu, wd):
        in_specs.append(pl.BlockSpec((1, 1) + w.shape[2:], lambda b, t: w_idx + (0, 0),
                                     pipeline_mode=pl.Buffered(1)))
    args += [wgu, wd]
    if final_g is not None:
        in_specs.append(_const_spec((1, D)))
        args.append(final_g.reshape(1, D))
    out_specs = [pl.BlockSpec((1, tm, D), tok)]
    out_shape = [jax.ShapeDtypeStruct(x.shape, F32)]
    if emit_idx is not None:
        out_specs.append(pl.BlockSpec((1, tm, D), tok))
        out_shape.append(jax.ShapeDtypeStruct(x.shape, emit_dtype))
    kern = functools.partial(
        _ffn_kernel, n_mix=len(mixes), mix_bias=mix_bias is not None, mod_idx=mod_idx,
        emit_idx=emit_idx, final_norm=final_g is not None)
    res = pl.pallas_call(
        kern,
        grid=(nb, nt // tm),
        in_specs=in_specs,
        out_specs=out_specs,
        out_shape=out_shape,
        compiler_params=pltpu.CompilerParams(
            dimension_semantics=("parallel", "parallel"), vmem_limit_bytes=VMEM_LIMIT),
        name="ffn",
    )(*args)
    return res if emit_idx is not None else res[0]


def _inproj_kernel(h_ref, w_ref, cos_ref, sa_ref, sb_ref, ng_ref, nb_ref, ws_ref, bs_ref,
                   q_ref, k_ref, v_ref, g_ref):
    h = h_ref[0]
    tm = h.shape[0]

    def rope(x, scale):
        up = pltpu.roll(x, HEAD_W - ROPE_FREQS, 1)
        dn = pltpu.roll(x, ROPE_FREQS, 1)
        r = x * cos_ref[...] + up * sa_ref[...] + dn * sb_ref[...]
        return r * scale if scale != 1.0 else r

    def project_rope(ref, col0, scale):
        p = _dot(h, w_ref[:, col0:col0 + QKV_W])
        for hd in range(HEADS):
            sl = slice(hd * HEAD_W, (hd + 1) * HEAD_W)
            ref[0, :, sl] = rope(p[:, sl], scale).astype(BF16)

    z0 = 3 * QKV_W
    vg = jax.nn.gelu(_dot(h, w_ref[:, z0 + GMLP_W:]), approximate=True)
    u = jax.nn.gelu(_dot(h, w_ref[:, z0:z0 + GMLP_W]), approximate=True)
    tn = []
    for grp in range(GMLP_W // GMLP_CHUNK):
        sl = slice(grp * GMLP_CHUNK, (grp + 1) * GMLP_CHUNK)
        t = vg[:, sl]
        mu = jnp.mean(t, axis=-1, keepdims=True)
        tc = t - mu
        var = jnp.mean(tc * tc, axis=-1, keepdims=True)
        tn.append((tc * lax.rsqrt(var + EPS) * ng_ref[:, sl] + nb_ref[:, sl]).astype(BF16))
    project_rope(q_ref, 0, Q_SCALE)
    for grp in range(GMLP_W // GMLP_CHUNK):
        sl = slice(grp * GMLP_CHUNK, (grp + 1) * GMLP_CHUNK)
        for ch in range(tm // GMLP_CHUNK):
            rows = slice(ch * GMLP_CHUNK, (ch + 1) * GMLP_CHUNK)
            mixed = _dot(ws_ref[grp], tn[grp][rows]) + bs_ref[:, sl]
            g_ref[0, rows, sl] = (u[rows, sl] * mixed).astype(BF16)
    project_rope(k_ref, QKV_W, 1.0)
    v_ref[0] = _dot(h, w_ref[:, 2 * QKV_W:3 * QKV_W]).T.astype(BF16)


def _inproj(h, w_in, cos_t, sa_t, sb_t, norm_g, norm_b, w_s, b_s):
    nb, nt, _ = h.shape
    tm = INPROJ_TILE
    tok = lambda b, t: (b, t, 0)
    tab = pl.BlockSpec((tm, HEAD_W), lambda b, t: (t, 0))
    out = jax.ShapeDtypeStruct((nb, nt, QKV_W), BF16)
    out_t = jax.ShapeDtypeStruct((nb, QKV_W, nt), BF16)
    tok_spec = pl.BlockSpec((1, tm, QKV_W), tok)
    return pl.pallas_call(
        _inproj_kernel,
        grid=(nb, nt // tm),
        in_specs=[pl.BlockSpec((1, tm, D), tok), _const_spec(w_in.shape), tab, tab, tab,
                  _const_spec(norm_g.shape), _const_spec(norm_b.shape),
                  _const_spec(w_s.shape), _const_spec(b_s.shape)],
        out_specs=[tok_spec, tok_spec, pl.BlockSpec((1, QKV_W, tm), lambda b, t: (b, 0, t)), tok_spec],
        out_shape=[out, out, out_t, out],
        compiler_params=pltpu.CompilerParams(
            dimension_semantics=("parallel", "parallel"), vmem_limit_bytes=VMEM_LIMIT),
        name="inproj",
    )(h, w_in, cos_t, sa_t, sb_t, norm_g, norm_b, w_s, b_s)


def _ctx_kv_kernel(h_ref, w_ref, k_ref, v_ref):
    h = h_ref[0]
    k_ref[0] = _dot(h, w_ref[:, QKV_W:2 * QKV_W]).astype(BF16)
    v_ref[0] = _dot(h, w_ref[:, 2 * QKV_W:3 * QKV_W]).T.astype(BF16)


def _ctx_kv(h, w_in):
    nb, nt, _ = h.shape
    blk = lambda b: (b, 0, 0)
    return pl.pallas_call(
        _ctx_kv_kernel,
        grid=(nb,),
        in_specs=[pl.BlockSpec((1, nt, D), blk), _const_spec(w_in.shape)],
        out_specs=[pl.BlockSpec((1, nt, QKV_W), blk), pl.BlockSpec((1, QKV_W, nt), blk)],
        out_shape=[jax.ShapeDtypeStruct((nb, nt, QKV_W), BF16), jax.ShapeDtypeStruct((nb, QKV_W, nt), BF16)],
        compiler_params=pltpu.CompilerParams(vmem_limit_bytes=VMEM_LIMIT),
        name="ctx_kv",
    )(h, w_in)


def _rope_tables(seq):
    rows = seq // GRID_W
    row = jnp.repeat(jnp.arange(rows), GRID_W)
    col = jnp.tile(jnp.arange(GRID_W), rows)
    inv = ROPE_BASE ** (-jnp.arange(ROPE_FREQS, dtype=F32) / ROPE_FREQS)
    ang = jnp.stack([row[:, None] * inv, col[:, None] * inv], axis=1)
    cos, sin = jnp.cos(ang), jnp.sin(ang)
    zero = jnp.zeros_like(sin)
    cos_c = jnp.stack([cos, cos], axis=2).reshape(seq, COMP)
    sa_c = jnp.stack([-sin, zero], axis=2).reshape(seq, COMP)
    sb_c = jnp.stack([zero, sin], axis=2).reshape(seq, COMP)
    two = lambda t: jnp.concatenate([t, t], axis=-1)
    return two(cos_c), two(sa_c), two(sb_c)


def _attn_kernel(q_ref, k_ref, vt_ref, kc_ref, vct_ref, lam_ref, g_ref, o_ref, acc_scr, *, lam_init):
    @pl.when(pl.program_id(0) == 0)
    def _():
        acc_scr[...] = jnp.ones_like(acc_scr)

    lv = lam_ref[...]
    lam = (jnp.exp(jnp.sum(lv[0:1] * lv[1:2], axis=-1, keepdims=True))
           - jnp.exp(jnp.sum(lv[2:3] * lv[3:4], axis=-1, keepdims=True)) + lam_init)
    tq = q_ref.shape[1]
    sq = Q_STREAM
    nt_dims = (((1,), (1,)), ((), ()))
    chunks = [(kr, vr, slice(k0, min(k0 + KEY_CHUNK, kr.shape[1])))
              for kr, vr in ((k_ref, vt_ref), (kc_ref, vct_ref)) for k0 in range(0, kr.shape[1], KEY_CHUNK)]
    lane = lax.broadcasted_iota(jnp.int32, (sq, HEAD_W), 1)

    qq = []
    for j in range(tq // sq):
        qh = q_ref[0, j * sq:(j + 1) * sq, :]
        zero = jnp.zeros_like(qh)
        qq.append(jnp.concatenate([jnp.where(lane < COMP, qh, zero),
                                   jnp.where(lane >= COMP, qh, zero)], axis=0))

    for j in range(tq // sq):
        prev = acc_scr[j]
        r = 1.0 / prev[HEAD_W:HEAD_W + 1]
        o_t = prev[:HEAD_W, :sq] * r[:, :sq] - prev[:HEAD_W, sq:] * (lam * r[:, sq:])
        inv = lax.rsqrt(jnp.mean(o_t * o_t, axis=0, keepdims=True) + EPS)
        y = (o_t * inv).T * (g_ref[...] * (1.0 - lam_init))
        o_ref[0, j * sq:(j + 1) * sq, :] = y.astype(BF16)

    def scores(chunk, j):
        kr, _, rows = chunk
        return lax.dot_general(kr[0, rows, :], qq[j], nt_dims, preferred_element_type=F32)

    n_streams = tq // sq
    m = [jnp.full((1, 2 * sq), -1e30, F32)] * n_streams
    acc = [jnp.zeros((HEAD_W + ONES_ROWS, 2 * sq), F32)] * n_streams
    s_cur = [scores(chunks[0], j) for j in range(n_streams)]
    for c, (_, vr, rows) in enumerate(chunks):
        ones = jnp.ones((ONES_ROWS, rows.stop - rows.start), BF16)
        vt = jnp.concatenate([vr[0, :, rows], ones], axis=0)
        for j in range(n_streams):
            s = s_cur[j]
            if c + 1 < len(chunks):
                s_cur[j] = scores(chunks[c + 1], j)
            m_new = jnp.maximum(m[j], jnp.max(s, axis=0, keepdims=True))
            e = jnp.exp2((s - m_new).astype(BF16))
            acc[j] = acc[j] * jnp.exp2(m[j] - m_new) + _dot(vt, e)
            m[j] = m_new
    for j in range(n_streams):
        acc_scr[j] = acc[j]


def _attn(q, k, vt, kc, vct, lam_vec, subln_g, lam_init):
    nb, nt, _ = q.shape
    nc = kc.shape[1]
    tq = ATTN_Q_TILE
    assert nt % KEY_CHUNK == 0 and nt % tq == 0
    n_t = nt // tq
    n_tiles = nb * HEADS * n_t

    def tile(i):
        return i // (HEADS * n_t), (i // n_t) % HEADS, i % n_t

    def cur(i):
        return tile(jnp.minimum(i, n_tiles - 1))

    def prev(i):
        return tile(jnp.maximum(i - 1, 0))

    keys = lambda n: pl.BlockSpec((1, n, HEAD_W), lambda i: (cur(i)[0], 0, cur(i)[1]))
    vals = lambda n: pl.BlockSpec((1, HEAD_W, n), lambda i: (cur(i)[0], cur(i)[1], 0))
    return pl.pallas_call(
        functools.partial(_attn_kernel, lam_init=lam_init),
        grid=(n_tiles + 1,),
        in_specs=[pl.BlockSpec((1, tq, HEAD_W), lambda i: (cur(i)[0], cur(i)[2], cur(i)[1])),
                  keys(nt), vals(nt), keys(nc), vals(nc),
                  _const_spec(lam_vec.shape), _const_spec((1, HEAD_W))],
        out_specs=pl.BlockSpec((1, tq, HEAD_W), lambda i: (prev(i)[0], prev(i)[2], prev(i)[1])),
        out_shape=jax.ShapeDtypeStruct((nb, nt, QKV_W), BF16),
        scratch_shapes=[pltpu.VMEM((tq // Q_STREAM, HEAD_W + ONES_ROWS, 2 * Q_STREAM), F32)],
        compiler_params=pltpu.CompilerParams(
            dimension_semantics=("arbitrary",), vmem_limit_bytes=VMEM_LIMIT),
        name="attn",
    )(q, k, vt, kc, vct, lam_vec, subln_g.reshape(1, HEAD_W))


def _fourier_consts():
    r, s, gc = DFT_R, SUB, FOURIER_GROUP_CH
    seq = r ** 3
    idx = np.arange(r)
    ang = 2.0 * np.pi * np.outer(idx, idx) / r
    c16, s16 = np.cos(ang), np.sin(ang)
    eye = np.eye(s)
    ma = np.zeros((2, 2, s, s, r, s))
    for ka3 in range(2):
        for kal in range(s):
            for cl in range(s):
                ma[0, ka3, cl, kal, :, cl] = c16[8 * ka3 + kal]
                ma[1, ka3, cl, kal, :, cl] = -s16[8 * ka3 + kal]
    ma = ma.reshape(2 * r * s, r * s)
    kc_, ks_ = np.kron(c16, eye), np.kron(s16, eye)
    mb = np.block([[kc_, ks_], [-ks_, kc_]])
    cc = np.zeros((r, s, s, 2, s))
    sc = np.zeros((r, s, s, 2, s))
    for cl in range(s):
        for c3 in range(2):
            for kal in range(s):
                cc[:, kal, cl, c3, kal] = c16[:, 8 * c3 + cl]
                sc[:, kal, cl, c3, kal] = s16[:, 8 * c3 + cl]
    cc, sc = cc.reshape(r * s, r * s), sc.reshape(r * s, r * s)
    mc = np.block([[cc, sc], [-sc, cc]])
    b_, c3_, ka3_, cl_, kal_ = np.meshgrid(idx, np.arange(2), np.arange(2), np.arange(s), np.arange(s), indexing="ij")
    t1 = (2.0 * np.pi * (16 * b_ + 8 * c3_ + cl_) * (8 * ka3_ + kal_) / seq).reshape(2 * r, r * s)
    ka3_, cl_, c3_, kb_, kal_ = np.meshgrid(np.arange(2), np.arange(s), np.arange(2), idx, np.arange(s), indexing="ij")
    t2 = (2.0 * np.pi * (8 * c3_ + cl_) * kb_ / (r * r) + 0.0 * (ka3_ + kal_)).reshape(2 * r, r * s)
    lanes = lambda t: np.broadcast_to(t[:, :, None], t.shape + (128,))
    ach = 2.0 * np.pi * np.outer(np.arange(gc), np.arange(gc)) / gc
    cd = np.concatenate([np.cos(ach), np.sin(ach)], axis=0) / math.sqrt(seq * gc)
    bf = lambda t: jnp.asarray(t, dtype=F32).astype(BF16)
    f32 = lambda t: jnp.asarray(t, dtype=F32)
    return (bf(ma), bf(mb), bf(mc), f32(lanes(np.cos(t1))), f32(lanes(np.sin(t1))),
            f32(lanes(np.cos(t2))), f32(lanes(np.sin(t2))), bf(cd))


def _fourier_kernel(h_ref, ma_ref, mb_ref, mc_ref, c1_ref, s1_ref, c2_ref, s2_ref, cd_ref, o_ref,
                    sa_scr, sb_scr):
    r, s, gc = DFT_R, SUB, FOURIER_GROUP_CH
    rows = r * s
    wide = lambda t: jnp.concatenate([t, t], axis=1)

    def twiddle(y, c_ref, s_ref, g):
        yr, yi = y[:rows], y[rows:]
        tc, ts = wide(c_ref[g]), wide(s_ref[g])
        return yr * tc + yi * ts, yi * tc - yr * ts

    for b in range(r):
        for c3 in range(2):
            g = 2 * b + c3
            x = h_ref[0, :, g, :, :].reshape(rows, gc).astype(BF16)
            re, im = twiddle(_dot(ma_ref[...], x), c1_ref, s1_ref, g)
            sa_scr[0, :, :, b, c3] = re.reshape(2, s, s, gc)
            sa_scr[1, :, :, b, c3] = im.reshape(2, s, s, gc)
    for ka3 in range(2):
        for cl in range(s):
            for c3 in range(2):
                g = (ka3 * s + cl) * 2 + c3
                x = jnp.concatenate([sa_scr[0, ka3, cl, :, c3].reshape(rows, gc),
                                     sa_scr[1, ka3, cl, :, c3].reshape(rows, gc)], axis=0).astype(BF16)
                re, im = twiddle(_dot(mb_ref[...], x), c2_ref, s2_ref, g)
                sb_scr[0, ka3, cl, :, c3] = re.reshape(r, s, gc)
                sb_scr[1, ka3, cl, :, c3] = im.reshape(r, s, gc)
    for ka3 in range(2):
        for kb in range(r):
            x = jnp.concatenate([sb_scr[0, ka3, :, kb, :].reshape(rows, gc),
                                 sb_scr[1, ka3, :, kb, :].reshape(rows, gc)], axis=0).astype(BF16)
            f = _dot(mc_ref[...], x)
            fri = jnp.concatenate([f[:rows], f[rows:]], axis=1).astype(BF16)
            o_ref[0, :, kb, ka3] = _dot(fri, cd_ref[...]).reshape(r, s, gc)


def _fourier(h, consts):
    nb, seq, _ = h.shape
    r, s, gc = DFT_R, SUB, FOURIER_GROUP_CH
    h5 = h.reshape(nb, r, 2 * r, s, D)
    stage_scratch = pltpu.VMEM((2, 2, s, r, 2, s, gc), F32)
    out = pl.pallas_call(
        _fourier_kernel,
        grid=(nb, D // gc),
        in_specs=[pl.BlockSpec((1, r, 2 * r, s, gc), lambda b, c: (b, 0, 0, 0, c))]
        + [_const_spec(a.shape) for a in consts],
        out_specs=pl.BlockSpec((1, r, r, 2, s, gc), lambda b, c: (b, 0, 0, 0, 0, c)),
        out_shape=jax.ShapeDtypeStruct((nb, r, r, 2, s, D), F32),
        scratch_shapes=[stage_scratch, stage_scratch],
        compiler_params=pltpu.CompilerParams(
            dimension_semantics=("parallel", "parallel"), vmem_limit_bytes=VMEM_LIMIT),
        name="fourier",
    )(h5, *consts)
    return out.reshape(nb, seq, D)


def kernel(x, c, ctx, c_ctx, ada_w, ada_b, ffn_w_gu, ffn_w_down, mix_w_in, mix_w_out, diff_lambda,
           diff_subln_g, sgu_norm_g, sgu_norm_b, sgu_w, sgu_b, fourier_w_out, fourier_b_out, final_norm_g):
    nb, seq, _ = x.shape
    assert seq == DFT_R ** 3 and DEPTH == 2
    ctx_row = nb
    cc = jnp.concatenate([c, c_ctx[None, :], jnp.zeros((2 * SUB - nb - 1, D), F32)], axis=0)
    mods = _ada(cc, ada_w, ada_b).reshape(DEPTH, cc.shape[0], N_MOD, D)
    latent = lambda b: b
    shared = lambda b: ctx_row

    ffn_w = (ffn_w_gu.astype(BF16), ffn_w_down.astype(BF16))

    x, h = _ffn(x, mods[0], latent, *ffn_w, (0, 0), mod_idx=(0, 1, 2), emit_idx=(3, 4))
    _, hc = _ffn(ctx.reshape(1, -1, D), mods[0], shared, *ffn_w, (0, 0), mod_idx=(0, 1, 2), emit_idx=(3, 4))
    hc = hc.reshape(ctx.shape)

    w_in = mix_w_in[0].astype(BF16)
    cos_t, sa_t, sb_t = _rope_tables(seq)
    bs_rows = jnp.repeat(sgu_b[0].T, GMLP_CHUNK, axis=1)
    q, k, vt, gated = _inproj(h, w_in, cos_t, sa_t, sb_t,
                             sgu_norm_g[0].reshape(1, GMLP_W), sgu_norm_b[0].reshape(1, GMLP_W),
                             sgu_w[0].astype(BF16), bs_rows)
    kc, vct = _ctx_kv(hc, w_in)
    lam_init = 0.8 - 0.6 * math.exp(-0.3 * 0)
    att = _attn(q, k, vt, kc, vct,
                diff_lambda[0], diff_subln_g[0], lam_init)
    w_out = mix_w_out[0].astype(BF16)
    x = _ffn(x, mods[0], latent, *ffn_w, (0, 1), mod_idx=(6, 7, 8),
             mixes=((att, w_out[:QKV_W]), (gated, w_out[QKV_W:])))

    x, h = _ffn(x, mods[1], latent, *ffn_w, (1, 0), mod_idx=(0, 1, 2), emit_idx=(3, 4), emit_dtype=F32)
    f = _fourier(h, _fourier_consts())
    return _ffn(x, mods[1], latent, *ffn_w, (1, 1), mod_idx=(6, 7, 8),
                mixes=((f, fourier_w_out[0].astype(BF16)),), mix_bias=fourier_b_out[0],
                final_g=final_norm_g)
```

```python
import functools
import math

import numpy as np
import jax
import jax.numpy as jnp
from jax import lax
from jax.experimental import pallas as pl
from jax.experimental.pallas import tpu as pltpu

D = 1024
DEPTH = 2
N_MOD = 9
D_FF = 2816
EPS = 1e-6
GRID_W = 64
HEADS = 4
HEAD_W = 128
COMP = 64
ROPE_FREQS = 16
ROPE_BASE = 10000.0
QKV_W = HEADS * HEAD_W
GMLP_W = 512
GMLP_CHUNK = 128
IN_COLS = 3 * QKV_W + 2 * GMLP_W
FOURIER_GROUP_CH = 256
DFT_R = 16
SUB = 8

VMEM_LIMIT = 56 * 1024 * 1024
FF_CHUNK = 256
TOKEN_TILE = 512
INPROJ_TILE = 1024
ATTN_Q_TILE = 1024
KEY_CHUNK = 256
ONES_ROWS = 16
Q_STREAM = 128
Q_SCALE = COMP ** -0.5 * math.log2(math.e)

BF16 = jnp.bfloat16
F32 = jnp.float32


def _const_spec(shape):
    zeros = (0,) * len(shape)
    return pl.BlockSpec(shape, lambda *_: zeros, pipeline_mode=pl.Buffered(1))


def _rms(x):
    return x * lax.rsqrt(jnp.mean(x * x, axis=-1, keepdims=True) + EPS)


def _dot(a, b):
    return jnp.dot(a, b, preferred_element_type=F32)


def _ada_kernel(c_ref, w_ref, b_ref, o_ref):
    s = jax.nn.silu(c_ref[...]).astype(BF16)
    o_ref[0] = _dot(s, w_ref[0].astype(BF16)) + b_ref[0]


def _ada(cc, ada_w, ada_b):
    rows = cc.shape[0]
    tn = 1024
    return pl.pallas_call(
        _ada_kernel,
        grid=(DEPTH, N_MOD * D // tn),
        in_specs=[
            pl.BlockSpec((rows, D), lambda l, j: (0, 0)),
            pl.BlockSpec((1, D, tn), lambda l, j: (l, 0, j)),
            pl.BlockSpec((1, 1, tn), lambda l, j: (l, 0, j)),
        ],
        out_specs=pl.BlockSpec((1, rows, tn), lambda l, j: (l, 0, j)),
        out_shape=jax.ShapeDtypeStruct((DEPTH, rows, N_MOD * D), F32),
        name="ada",
    )(cc, ada_w, ada_b.reshape(DEPTH, 1, N_MOD * D))


def _ffn_kernel(*refs, n_mix, mix_bias, mod_idx, emit_idx, final_norm):
    refs = list(refs)
    x_ref, mod_ref = refs[:2]
    pos = 2
    mix_refs = refs[pos:pos + 2 * n_mix]
    pos += 2 * n_mix
    bmix_ref = None
    if mix_bias:
        bmix_ref = refs[pos]
        pos += 1
    wgu_ref, wd_ref = refs[pos:pos + 2]
    pos += 2
    fng_ref = None
    if final_norm:
        fng_ref = refs[pos]
        pos += 1
    out_ref = refs[pos]
    pos += 1
    hn_ref = None
    if emit_idx is not None:
        hn_ref = refs[pos]

    def mod(i):
        return mod_ref[0, i:i + 1, :]

    x = x_ref[0]
    if n_mix:
        y = None
        for m in range(n_mix):
            t = _dot(mix_refs[2 * m][0].astype(BF16), mix_refs[2 * m + 1][...])
            y = t if y is None else y + t
        if mix_bias:
            y = y + bmix_ref[...]
        x = x + mod(5) * y

    sh, sc, gt = mod_idx
    h = (_rms(x) * (1.0 + mod(sc)) + mod(sh)).astype(BF16)
    acc = None
    for f in range(D_FF // FF_CHUNK):
        cols = slice(f * FF_CHUNK, (f + 1) * FF_CHUNK)
        up_cols = slice(D_FF + f * FF_CHUNK, D_FF + (f + 1) * FF_CHUNK)
        g = _dot(h, wgu_ref[0, 0, :, cols].astype(BF16))
        u = _dot(h, wgu_ref[0, 0, :, up_cols].astype(BF16))
        a = (jax.nn.silu(g) * u).astype(BF16)
        t = _dot(a, wd_ref[0, 0, cols, :].astype(BF16))
        acc = t if acc is None else acc + t

    out = x + (0.5 * mod(gt)) * acc
    if emit_idx is not None:
        esh, esc = emit_idx
        hn_ref[0] = (_rms(out) * (1.0 + mod(esc)) + mod(esh)).astype(hn_ref.dtype)
    if final_norm:
        out = _rms(out) * fng_ref[...]
    out_ref[0] = out


def _ffn(x, mod_l, mod_row, wgu, wd, w_idx, *, mod_idx, mixes=(), mix_bias=None,
         emit_idx=None, emit_dtype=BF16, final_g=None):
    nb, nt, _ = x.shape
    tm = min(TOKEN_TILE, nt)
    tok = lambda b, t: (b, t, 0)
    in_specs = [
        pl.BlockSpec((1, tm, D), tok),
        pl.BlockSpec((1, N_MOD, D), lambda b, t: (mod_row(b), 0, 0)),
    ]
    args = [x, mod_l]
    for arr, w in mixes:
        in_specs += [pl.BlockSpec((1, tm, arr.shape[-1]), tok), _const_spec(w.shape)]
        args += [arr, w]
    if mix_bias is not None:
        in_specs.append(_const_spec((1, D)))
        args.append(mix_bias.reshape(1, D))
    for w in (wgu, wd):
        in_specs.append(pl.BlockSpec((1, 1) + w.shape[2:], lambda b, t: w_idx + (0, 0),
                                     pipeline_mode=pl.Buffered(1)))
    args += [wgu, wd]
    if final_g is not None:
        in_specs.append(_const_spec((1, D)))
        args.append(final_g.reshape(1, D))
    out_specs = [pl.BlockSpec((1, tm, D), tok)]
    out_shape = [jax.ShapeDtypeStruct(x.shape, F32)]
    if emit_idx is not None:
        out_specs.append(pl.BlockSpec((1, tm, D), tok))
        out_shape.append(jax.ShapeDtypeStruct(x.shape, emit_dtype))
    kern = functools.partial(
        _ffn_kernel, n_mix=len(mixes), mix_bias=mix_bias is not None, mod_idx=mod_idx,
        emit_idx=emit_idx, final_norm=final_g is not None)
    res = pl.pallas_call(
        kern,
        grid=(nb, nt // tm),
        in_specs=in_specs,
        out_specs=out_specs,
        out_shape=out_shape,
        compiler_params=pltpu.CompilerParams(
            dimension_semantics=("parallel", "parallel"), vmem_limit_bytes=VMEM_LIMIT),
        name="ffn",
    )(*args)
    return res if emit_idx is not None else res[0]


def _inproj_kernel(h_ref, w_ref, cos_ref, sa_ref, sb_ref, ng_ref, nb_ref, ws_ref, bs_ref,
                   q_ref, k_ref, v_ref, g_ref):
    h = h_ref[0]
    tm = h.shape[0]

    def rope(x, scale):
        up = pltpu.roll(x, HEAD_W - ROPE_FREQS, 1)
        dn = pltpu.roll(x, ROPE_FREQS, 1)
        r = x * cos_ref[...] + up * sa_ref[...] + dn * sb_ref[...]
        return r * scale if scale != 1.0 else r

    def project_rope(ref, col0, scale):
        p = _dot(h, w_ref[:, col0:col0 + QKV_W])
        for hd in range(HEADS):
            sl = slice(hd * HEAD_W, (hd + 1) * HEAD_W)
            ref[0, :, sl] = rope(p[:, sl], scale).astype(BF16)

    z0 = 3 * QKV_W
    vg = jax.nn.gelu(_dot(h, w_ref[:, z0 + GMLP_W:]), approximate=True)
    u = jax.nn.gelu(_dot(h, w_ref[:, z0:z0 + GMLP_W]), approximate=True)
    tn = []
    for grp in range(GMLP_W // GMLP_CHUNK):
        sl = slice(grp * GMLP_CHUNK, (grp + 1) * GMLP_CHUNK)
        t = vg[:, sl]
        mu = jnp.mean(t, axis=-1, keepdims=True)
        tc = t - mu
        var = jnp.mean(tc * tc, axis=-1, keepdims=True)
        tn.append((tc * lax.rsqrt(var + EPS) * ng_ref[:, sl] + nb_ref[:, sl]).astype(BF16))
    project_rope(q_ref, 0, Q_SCALE)
    for grp in range(GMLP_W // GMLP_CHUNK):
        sl = slice(grp * GMLP_CHUNK, (grp + 1) * GMLP_CHUNK)
        for ch in range(tm // GMLP_CHUNK):
            rows = slice(ch * GMLP_CHUNK, (ch + 1) * GMLP_CHUNK)
            mixed = _dot(ws_ref[grp], tn[grp][rows]) + bs_ref[:, sl]
            g_ref[0, rows, sl] = (u[rows, sl] * mixed).astype(BF16)
    project_rope(k_ref, QKV_W, 1.0)
    v_ref[0] = _dot(h, w_ref[:, 2 * QKV_W:3 * QKV_W]).T.astype(BF16)


def _inproj(h, w_in, cos_t, sa_t, sb_t, norm_g, norm_b, w_s, b_s):
    nb, nt, _ = h.shape
    tm = INPROJ_TILE
    tok = lambda b, t: (b, t, 0)
    tab = pl.BlockSpec((tm, HEAD_W), lambda b, t: (t, 0))
    out = jax.ShapeDtypeStruct((nb, nt, QKV_W), BF16)
    out_t = jax.ShapeDtypeStruct((nb, QKV_W, nt), BF16)
    tok_spec = pl.BlockSpec((1, tm, QKV_W), tok)
    return pl.pallas_call(
        _inproj_kernel,
        grid=(nb, nt // tm),
        in_specs=[pl.BlockSpec((1, tm, D), tok), _const_spec(w_in.shape), tab, tab, tab,
                  _const_spec(norm_g.shape), _const_spec(norm_b.shape),
                  _const_spec(w_s.shape), _const_spec(b_s.shape)],
        out_specs=[tok_spec, tok_spec, pl.BlockSpec((1, QKV_W, tm), lambda b, t: (b, 0, t)), tok_spec],
        out_shape=[out, out, out_t, out],
        compiler_params=pltpu.CompilerParams(
            dimension_semantics=("parallel", "parallel"), vmem_limit_bytes=VMEM_LIMIT),
        name="inproj",
    )(h, w_in, cos_t, sa_t, sb_t, norm_g, norm_b, w_s, b_s)


def _ctx_kv_kernel(h_ref, w_ref, k_ref, v_ref):
    h = h_ref[0]
    k_ref[0] = _dot(h, w_ref[:, QKV_W:2 * QKV_W]).astype(BF16)
    v_ref[0] = _dot(h, w_ref[:, 2 * QKV_W:3 * QKV_W]).T.astype(BF16)


def _ctx_kv(h, w_in):
    nb, nt, _ = h.shape
    blk = lambda b: (b, 0, 0)
    return pl.pallas_call(
        _ctx_kv_kernel,
        grid=(nb,),
        in_specs=[pl.BlockSpec((1, nt, D), blk), _const_spec(w_in.shape)],
        out_specs=[pl.BlockSpec((1, nt, QKV_W), blk), pl.BlockSpec((1, QKV_W, nt), blk)],
        out_shape=[jax.ShapeDtypeStruct((nb, nt, QKV_W), BF16), jax.ShapeDtypeStruct((nb, QKV_W, nt), BF16)],
        compiler_params=pltpu.CompilerParams(vmem_limit_bytes=VMEM_LIMIT),
        name="ctx_kv",
    )(h, w_in)


def _rope_tables(seq):
    rows = seq // GRID_W
    row = jnp.repeat(jnp.arange(rows), GRID_W)
    col = jnp.tile(jnp.arange(GRID_W), rows)
    inv = ROPE_BASE ** (-jnp.arange(ROPE_FREQS, dtype=F32) / ROPE_FREQS)
    ang = jnp.stack([row[:, None] * inv, col[:, None] * inv], axis=1)
    cos, sin = jnp.cos(ang), jnp.sin(ang)
    zero = jnp.zeros_like(sin)
    cos_c = jnp.stack([cos, cos], axis=2).reshape(seq, COMP)
    sa_c = jnp.stack([-sin, zero], axis=2).reshape(seq, COMP)
    sb_c = jnp.stack([zero, sin], axis=2).reshape(seq, COMP)
    two = lambda t: jnp.concatenate([t, t], axis=-1)
    return two(cos_c), two(sa_c), two(sb_c)


def _attn_kernel(q_ref, k_ref, vt_ref, kc_ref, vct_ref, lam_ref, g_ref, o_ref, acc_scr, *, lam_init):
    @pl.when(pl.program_id(0) == 0)
    def _():
        acc_scr[...] = jnp.ones_like(acc_scr)

    lv = lam_ref[...]
    lam = (jnp.exp(jnp.sum(lv[0:1] * lv[1:2], axis=-1, keepdims=True))
           - jnp.exp(jnp.sum(lv[2:3] * lv[3:4], axis=-1, keepdims=True)) + lam_init)
    tq = q_ref.shape[1]
    sq = Q_STREAM
    nt_dims = (((1,), (1,)), ((), ()))
    chunks = [(kr, vr, slice(k0, min(k0 + KEY_CHUNK, kr.shape[1])))
              for kr, vr in ((k_ref, vt_ref), (kc_ref, vct_ref)) for k0 in range(0, kr.shape[1], KEY_CHUNK)]
    lane = lax.broadcasted_iota(jnp.int32, (sq, HEAD_W), 1)

    qq = []
    for j in range(tq // sq):
        qh = q_ref[0, j * sq:(j + 1) * sq, :]
        zero = jnp.zeros_like(qh)
        qq.append(jnp.concatenate([jnp.where(lane < COMP, qh, zero),
                                   jnp.where(lane >= COMP, qh, zero)], axis=0))

    for j in range(tq // sq):
        prev = acc_scr[j]
        r = 1.0 / prev[HEAD_W:HEAD_W + 1]
        o_t = prev[:HEAD_W, :sq] * r[:, :sq] - prev[:HEAD_W, sq:] * (lam * r[:, sq:])
        inv = lax.rsqrt(jnp.mean(o_t * o_t, axis=0, keepdims=True) + EPS)
        y = (o_t * inv).T * (g_ref[...] * (1.0 - lam_init))
        o_ref[0, j * sq:(j + 1) * sq, :] = y.astype(BF16)

    def scores(chunk, j):
        kr, _, rows = chunk
        return lax.dot_general(kr[0, rows, :], qq[j], nt_dims, preferred_element_type=F32)

    n_streams = tq // sq
    m = [jnp.full((1, 2 * sq), -1e30, F32)] * n_streams
    acc = [jnp.zeros((HEAD_W + ONES_ROWS, 2 * sq), F32)] * n_streams
    s_cur = [scores(chunks[0], j) for j in range(n_streams)]
    for c, (_, vr, rows) in enumerate(chunks):
        ones = jnp.ones((ONES_ROWS, rows.stop - rows.start), BF16)
        vt = jnp.concatenate([vr[0, :, rows], ones], axis=0)
        for j in range(n_streams):
            s = s_cur[j]
            if c + 1 < len(chunks):
                s_cur[j] = scores(chunks[c + 1], j)
            m_new = jnp.maximum(m[j], jnp.max(s, axis=0, keepdims=True))
            e = jnp.exp2((s - m_new).astype(BF16))
            acc[j] = acc[j] * jnp.exp2(m[j] - m_new) + _dot(vt, e)
            m[j] = m_new
    for j in range(n_streams):
        acc_scr[j] = acc[j]


def _attn(q, k, vt, kc, vct, lam_vec, subln_g, lam_init):
    nb, nt, _ = q.shape
    nc = kc.shape[1]
    tq = ATTN_Q_TILE
    assert nt % KEY_CHUNK == 0 and nt % tq == 0
    n_t = nt // tq
    n_tiles = nb * HEADS * n_t

    def tile(i):
        return i // (HEADS * n_t), (i // n_t) % HEADS, i % n_t

    def cur(i):
        return tile(jnp.minimum(i, n_tiles - 1))

    def prev(i):
        return tile(jnp.maximum(i - 1, 0))

    keys = lambda n: pl.BlockSpec((1, n, HEAD_W), lambda i: (cur(i)[0], 0, cur(i)[1]))
    vals = lambda n: pl.BlockSpec((1, HEAD_W, n), lambda i: (cur(i)[0], cur(i)[1], 0))
    return pl.pallas_call(
        functools.partial(_attn_kernel, lam_init=lam_init),
        grid=(n_tiles + 1,),
        in_specs=[pl.BlockSpec((1, tq, HEAD_W), lambda i: (cur(i)[0], cur(i)[2], cur(i)[1])),
                  keys(nt), vals(nt), keys(nc), vals(nc),
                  _const_spec(lam_vec.shape), _const_spec((1, HEAD_W))],
        out_specs=pl.BlockSpec((1, tq, HEAD_W), lambda i: (prev(i)[0], prev(i)[2], prev(i)[1])),
        out_shape=jax.ShapeDtypeStruct((nb, nt, QKV_W), BF16),
        scratch_shapes=[pltpu.VMEM((tq // Q_STREAM, HEAD_W + ONES_ROWS, 2 * Q_STREAM), F32)],
        compiler_params=pltpu.CompilerParams(
            dimension_semantics=("arbitrary",), vmem_limit_bytes=VMEM_LIMIT),
        name="attn",
    )(q, k, vt, kc, vct, lam_vec, subln_g.reshape(1, HEAD_W))


def _fourier_consts():
    r, s, gc = DFT_R, SUB, FOURIER_GROUP_CH
    seq = r ** 3
    idx = np.arange(r)
    ang = 2.0 * np.pi * np.outer(idx, idx) / r
    c16, s16 = np.cos(ang), np.sin(ang)
    eye = np.eye(s)
    interleave = lambda m: m.reshape(2, r, s, -1).transpose(1, 0, 2, 3).reshape(2 * r * s, -1)
    ma = np.zeros((2, 2, s, s, r, s))
    for ka3 in range(2):
        for kal in range(s):
            for cl in range(s):
                ma[0, ka3, cl, kal, :, cl] = c16[8 * ka3 + kal]
                ma[1, ka3, cl, kal, :, cl] = -s16[8 * ka3 + kal]
    ma = interleave(ma.reshape(2 * r * s, r * s))
    kc_, ks_ = np.kron(c16, eye), np.kron(s16, eye)
    mb = interleave(np.block([[kc_, ks_], [-ks_, kc_]]))
    cc = np.zeros((r, s, s, 2, s))
    sc = np.zeros((r, s, s, 2, s))
    for cl in range(s):
        for c3 in range(2):
            for kal in range(s):
                cc[:, kal, cl, c3, kal] = c16[:, 8 * c3 + cl]
                sc[:, kal, cl, c3, kal] = s16[:, 8 * c3 + cl]
    cc, sc = cc.reshape(r * s, r * s), sc.reshape(r * s, r * s)
    mc = interleave(np.block([[cc, sc], [-sc, cc]]))
    def cdiag(theta):
        c, sn = np.diag(np.cos(theta)), np.diag(np.sin(theta))
        return np.block([[c, sn], [-sn, c]])

    mbg = np.zeros((2, s, 2, 2 * r * s, 2 * r * s))
    b_, kal_ = np.meshgrid(idx, np.arange(s), indexing="ij")
    for ka3 in range(2):
        for cl in range(s):
            for c3 in range(2):
                theta = 2.0 * np.pi * (16 * b_ + 8 * c3 + cl) * (8 * ka3 + kal_) / seq
                mbg[ka3, cl, c3] = mb @ cdiag(theta.reshape(-1))
    mbg = mbg.reshape(2 * r, 2 * r * s, 2 * r * s)
    mcg = np.zeros((r, 2 * r * s, 2 * r * s))
    cl_, c3_, kal_ = np.meshgrid(np.arange(s), np.arange(2), np.arange(s), indexing="ij")
    for kb in range(r):
        theta = 2.0 * np.pi * (8 * c3_ + cl_) * kb / (r * r) + 0.0 * kal_
        mcg[kb] = mc @ cdiag(theta.reshape(-1))
    ach = 2.0 * np.pi * np.outer(np.arange(gc), np.arange(gc)) / gc
    cd = np.concatenate([np.cos(ach), np.sin(ach)], axis=0) / math.sqrt(seq * gc)
    bf = lambda t: jnp.asarray(t, dtype=F32).astype(BF16)
    return bf(ma), bf(mbg), bf(mcg), bf(cd)


def _fourier_kernel(h_ref, ma_ref, mb_ref, mc_ref, cd_ref, o_ref, sa_scr, sb_scr):
    r, s, gc = DFT_R, SUB, FOURIER_GROUP_CH
    rows = r * s

    def split(y):
        y4 = y.reshape(r, 2, s, gc)
        return y4[:, 0].reshape(rows, gc), y4[:, 1].reshape(rows, gc)

    for b in range(r):
        for c3 in range(2):
            x = h_ref[0, :, 2 * b + c3, :, :].reshape(rows, gc).astype(BF16)
            re, im = split(_dot(ma_ref[...], x))
            sa_scr[0, :, :, b, c3] = re.reshape(2, s, s, gc)
            sa_scr[1, :, :, b, c3] = im.reshape(2, s, s, gc)
    for ka3 in range(2):
        for cl in range(s):
            for c3 in range(2):
                g = (ka3 * s + cl) * 2 + c3
                x = jnp.concatenate([sa_scr[0, ka3, cl, :, c3].reshape(rows, gc),
                                     sa_scr[1, ka3, cl, :, c3].reshape(rows, gc)], axis=0).astype(BF16)
                re, im = split(_dot(mb_ref[g], x))
                sb_scr[0, ka3, cl, :, c3] = re.reshape(r, s, gc)
                sb_scr[1, ka3, cl, :, c3] = im.reshape(r, s, gc)
    for ka3 in range(2):
        for kb in range(r):
            x = jnp.concatenate([sb_scr[0, ka3, :, kb, :].reshape(rows, gc),
                                 sb_scr[1, ka3, :, kb, :].reshape(rows, gc)], axis=0).astype(BF16)
            f = _dot(mc_ref[kb], x)
            fri = jnp.concatenate(split(f), axis=1).astype(BF16)
            o_ref[0, :, kb, ka3] = _dot(fri, cd_ref[...]).reshape(r, s, gc)


def _fourier(h, consts):
    nb, seq, _ = h.shape
    r, s, gc = DFT_R, SUB, FOURIER_GROUP_CH
    h5 = h.reshape(nb, r, 2 * r, s, D)
    stage_scratch = pltpu.VMEM((2, 2, s, r, 2, s, gc), F32)
    out = pl.pallas_call(
        _fourier_kernel,
        grid=(nb, D // gc),
        in_specs=[pl.BlockSpec((1, r, 2 * r, s, gc), lambda b, c: (b, 0, 0, 0, c))]
        + [_const_spec(a.shape) for a in consts],
        out_specs=pl.BlockSpec((1, r, r, 2, s, gc), lambda b, c: (b, 0, 0, 0, 0, c)),
        out_shape=jax.ShapeDtypeStruct((nb, r, r, 2, s, D), F32),
        scratch_shapes=[stage_scratch, stage_scratch],
        compiler_params=pltpu.CompilerParams(
            dimension_semantics=("parallel", "parallel"), vmem_limit_bytes=VMEM_LIMIT),
        name="fourier",
    )(h5, *consts)
    return out.reshape(nb, seq, D)


def kernel(x, c, ctx, c_ctx, ada_w, ada_b, ffn_w_gu, ffn_w_down, mix_w_in, mix_w_out, diff_lambda,
           diff_subln_g, sgu_norm_g, sgu_norm_b, sgu_w, sgu_b, fourier_w_out, fourier_b_out, final_norm_g):
    nb, seq, _ = x.shape
    assert seq == DFT_R ** 3 and DEPTH == 2
    ctx_row = nb
    cc = jnp.concatenate([c, c_ctx[None, :], jnp.zeros((2 * SUB - nb - 1, D), F32)], axis=0)
    mods = _ada(cc, ada_w, ada_b).reshape(DEPTH, cc.shape[0], N_MOD, D)
    latent = lambda b: b
    shared = lambda b: ctx_row

    ffn_w = (ffn_w_gu, ffn_w_down)

    x, h = _ffn(x, mods[0], latent, *ffn_w, (0, 0), mod_idx=(0, 1, 2), emit_idx=(3, 4))
    _, hc = _ffn(ctx.reshape(1, -1, D), mods[0], shared, *ffn_w, (0, 0), mod_idx=(0, 1, 2), emit_idx=(3, 4))
    hc = hc.reshape(ctx.shape)

    w_in = mix_w_in[0].astype(BF16)
    cos_t, sa_t, sb_t = _rope_tables(seq)
    bs_rows = jnp.repeat(sgu_b[0].T, GMLP_CHUNK, axis=1)
    q, k, vt, gated = _inproj(h, w_in, cos_t, sa_t, sb_t,
                             sgu_norm_g[0].reshape(1, GMLP_W), sgu_norm_b[0].reshape(1, GMLP_W),
                             sgu_w[0].astype(BF16), bs_rows)
    kc, vct = _ctx_kv(hc, w_in)
    lam_init = 0.8 - 0.6 * math.exp(-0.3 * 0)
    att = _attn(q, k, vt, kc, vct,
                diff_lambda[0], diff_subln_g[0], lam_init)
    w_out = mix_w_out[0].astype(BF16)
    x = _ffn(x, mods[0], latent, *ffn_w, (0, 1), mod_idx=(6, 7, 8),
             mixes=((att, w_out[:QKV_W]), (gated, w_out[QKV_W:])))

    x, h = _ffn(x, mods[1], latent, *ffn_w, (1, 0), mod_idx=(0, 1, 2), emit_idx=(3, 4), emit_dtype=F32)
    f = _fourier(h, _fourier_consts())
    return _ffn(x, mods[1], latent, *ffn_w, (1, 1), mod_idx=(6, 7, 8),
                mixes=((f, fourier_w_out[0].astype(BF16)),), mix_bias=fourier_b_out[0],
                final_g=final_norm_g)
```

```python
import functools
import math

import numpy as np
import jax
import jax.numpy as jnp
from jax import lax
from jax.experimental import pallas as pl
from jax.experimental.pallas import tpu as pltpu

D = 1024
DEPTH = 2
N_MOD = 9
D_FF = 2816
EPS = 1e-6
GRID_W = 64
HEADS = 4
HEAD_W = 128
COMP = 64
ROPE_FREQS = 16
ROPE_BASE = 10000.0
QKV_W = HEADS * HEAD_W
GMLP_W = 512
GMLP_CHUNK = 128
IN_COLS = 3 * QKV_W + 2 * GMLP_W
FOURIER_GROUP_CH = 256
DFT_R = 16
SUB = 8

VMEM_LIMIT = 56 * 1024 * 1024
FF_CHUNK = 256
TOKEN_TILE = 512
INPROJ_TILE = 1024
ATTN_Q_TILE = 1024
KEY_CHUNK = 256
ONES_ROWS = 16
Q_STREAM = 128
Q_SCALE = COMP ** -0.5 * math.log2(math.e)

BF16 = jnp.bfloat16
F32 = jnp.float32


def _const_spec(shape):
    zeros = (0,) * len(shape)
    return pl.BlockSpec(shape, lambda *_: zeros, pipeline_mode=pl.Buffered(1))


def _rms(x):
    return x * lax.rsqrt(jnp.mean(x * x, axis=-1, keepdims=True) + EPS)


def _dot(a, b):
    return jnp.dot(a, b, preferred_element_type=F32)


def _ada_kernel(c_ref, w_ref, b_ref, o_ref):
    s = jax.nn.silu(c_ref[...]).astype(BF16)
    o_ref[0] = _dot(s, w_ref[0].astype(BF16)) + b_ref[0]


def _ada(cc, ada_w, ada_b):
    rows = cc.shape[0]
    tn = 1024
    return pl.pallas_call(
        _ada_kernel,
        grid=(DEPTH, N_MOD * D // tn),
        in_specs=[
            pl.BlockSpec((rows, D), lambda l, j: (0, 0)),
            pl.BlockSpec((1, D, tn), lambda l, j: (l, 0, j)),
            pl.BlockSpec((1, 1, tn), lambda l, j: (l, 0, j)),
        ],
        out_specs=pl.BlockSpec((1, rows, tn), lambda l, j: (l, 0, j)),
        out_shape=jax.ShapeDtypeStruct((DEPTH, rows, N_MOD * D), F32),
        name="ada",
    )(cc, ada_w, ada_b.reshape(DEPTH, 1, N_MOD * D))


def _ffn_kernel(*refs, n_mix, mix_bias, mod_idx, emit_idx, final_norm):
    refs = list(refs)
    x_ref, mod_ref = refs[:2]
    pos = 2
    mix_refs = refs[pos:pos + 2 * n_mix]
    pos += 2 * n_mix
    bmix_ref = None
    if mix_bias:
        bmix_ref = refs[pos]
        pos += 1
    wgu_ref, wd_ref = refs[pos:pos + 2]
    pos += 2
    fng_ref = None
    if final_norm:
        fng_ref = refs[pos]
        pos += 1
    out_ref = refs[pos]
    pos += 1
    hn_ref = None
    if emit_idx is not None:
        hn_ref = refs[pos]

    def mod(i):
        return mod_ref[0, i:i + 1, :]

    x = x_ref[0]
    if n_mix:
        y = None
        for m in range(n_mix):
            t = _dot(mix_refs[2 * m][0].astype(BF16), mix_refs[2 * m + 1][...])
            y = t if y is None else y + t
        if mix_bias:
            y = y + bmix_ref[...]
        x = x + mod(5) * y

    sh, sc, gt = mod_idx
    h = (_rms(x) * (1.0 + mod(sc)) + mod(sh)).astype(BF16)
    acc = None
    for f in range(D_FF // FF_CHUNK):
        cols = slice(f * FF_CHUNK, (f + 1) * FF_CHUNK)
        up_cols = slice(D_FF + f * FF_CHUNK, D_FF + (f + 1) * FF_CHUNK)
        g = _dot(h, wgu_ref[0, 0, :, cols].astype(BF16))
        u = _dot(h, wgu_ref[0, 0, :, up_cols].astype(BF16))
        a = (jax.nn.silu(g) * u).astype(BF16)
        t = _dot(a, wd_ref[0, 0, cols, :].astype(BF16))
        acc = t if acc is None else acc + t

    out = x + (0.5 * mod(gt)) * acc
    if emit_idx is not None:
        esh, esc = emit_idx
        hn_ref[0] = (_rms(out) * (1.0 + mod(esc)) + mod(esh)).astype(hn_ref.dtype)
    if final_norm:
        out = _rms(out) * fng_ref[...]
    out_ref[0] = out


def _ffn(x, mod_l, mod_row, wgu, wd, w_idx, *, mod_idx, mixes=(), mix_bias=None,
         emit_idx=None, emit_dtype=BF16, final_g=None):
    nb, nt, _ = x.shape
    tm = min(TOKEN_TILE, nt)
    tok = lambda b, t: (b, t, 0)
    in_specs = [
        pl.BlockSpec((1, tm, D), tok),
        pl.BlockSpec((1, N_MOD, D), lambda b, t: (mod_row(b), 0, 0)),
    ]
    args = [x, mod_l]
    for arr, w in mixes:
        in_specs += [pl.BlockSpec((1, tm, arr.shape[-1]), tok), _const_spec(w.shape)]
        args += [arr, w]
    if mix_bias is not None:
        in_specs.append(_const_spec((1, D)))
        args.append(mix_bias.reshape(1, D))
    for w in (wgu, wd):
        in_specs.append(pl.BlockSpec((1, 1) + w.shape[2:], lambda b, t: w_idx + (0, 0),
                                     pipeline_mode=pl.Buffered(1)))
    args += [wgu, wd]
    if final_g is not None:
        in_specs.append(_const_spec((1, D)))
        args.append(final_g.reshape(1, D))
    out_specs = [pl.BlockSpec((1, tm, D), tok)]
    out_shape = [jax.ShapeDtypeStruct(x.shape, F32)]
    if emit_idx is not None:
        out_specs.append(pl.BlockSpec((1, tm, D), tok))
        out_shape.append(jax.ShapeDtypeStruct(x.shape, emit_dtype))
    kern = functools.partial(
        _ffn_kernel, n_mix=len(mixes), mix_bias=mix_bias is not None, mod_idx=mod_idx,
        emit_idx=emit_idx, final_norm=final_g is not None)
    res = pl.pallas_call(
        kern,
        grid=(nb, nt // tm),
        in_specs=in_specs,
        out_specs=out_specs,
        out_shape=out_shape,
        compiler_params=pltpu.CompilerParams(
            dimension_semantics=("parallel", "parallel"), vmem_limit_bytes=VMEM_LIMIT),
        name="ffn",
    )(*args)
    return res if emit_idx is not None else res[0]


def _inproj_kernel(h_ref, w_ref, cos_ref, sa_ref, sb_ref, ng_ref, nb_ref, ws_ref, bs_ref,
                   q_ref, k_ref, v_ref, g_ref):
    h = h_ref[0]
    tm = h.shape[0]

    def rope(x, scale):
        up = pltpu.roll(x, HEAD_W - ROPE_FREQS, 1)
        dn = pltpu.roll(x, ROPE_FREQS, 1)
        r = x * cos_ref[...] + up * sa_ref[...] + dn * sb_ref[...]
        return r * scale if scale != 1.0 else r

    def project_rope(ref, col0, scale):
        p = _dot(h, w_ref[:, col0:col0 + QKV_W])
        for hd in range(HEADS):
            sl = slice(hd * HEAD_W, (hd + 1) * HEAD_W)
            ref[0, :, sl] = rope(p[:, sl], scale).astype(BF16)

    z0 = 3 * QKV_W
    vg = jax.nn.gelu(_dot(h, w_ref[:, z0 + GMLP_W:]), approximate=True)
    u = jax.nn.gelu(_dot(h, w_ref[:, z0:z0 + GMLP_W]), approximate=True)
    tn = []
    for grp in range(GMLP_W // GMLP_CHUNK):
        sl = slice(grp * GMLP_CHUNK, (grp + 1) * GMLP_CHUNK)
        t = vg[:, sl]
        mu = jnp.mean(t, axis=-1, keepdims=True)
        tc = t - mu
        var = jnp.mean(tc * tc, axis=-1, keepdims=True)
        tn.append((tc * lax.rsqrt(var + EPS) * ng_ref[:, sl] + nb_ref[:, sl]).astype(BF16))
    project_rope(q_ref, 0, Q_SCALE)
    for grp in range(GMLP_W // GMLP_CHUNK):
        sl = slice(grp * GMLP_CHUNK, (grp + 1) * GMLP_CHUNK)
        for ch in range(tm // GMLP_CHUNK):
            rows = slice(ch * GMLP_CHUNK, (ch + 1) * GMLP_CHUNK)
            mixed = _dot(ws_ref[grp], tn[grp][rows]) + bs_ref[:, sl]
            g_ref[0, rows, sl] = (u[rows, sl] * mixed).astype(BF16)
    project_rope(k_ref, QKV_W, 1.0)
    v_ref[0] = _dot(h, w_ref[:, 2 * QKV_W:3 * QKV_W]).T.astype(BF16)


def _inproj(h, w_in, cos_t, sa_t, sb_t, norm_g, norm_b, w_s, b_s):
    nb, nt, _ = h.shape
    tm = INPROJ_TILE
    tok = lambda b, t: (b, t, 0)
    tab = pl.BlockSpec((tm, HEAD_W), lambda b, t: (t, 0))
    out = jax.ShapeDtypeStruct((nb, nt, QKV_W), BF16)
    out_t = jax.ShapeDtypeStruct((nb, QKV_W, nt), BF16)
    tok_spec = pl.BlockSpec((1, tm, QKV_W), tok)
    return pl.pallas_call(
        _inproj_kernel,
        grid=(nb, nt // tm),
        in_specs=[pl.BlockSpec((1, tm, D), tok), _const_spec(w_in.shape), tab, tab, tab,
                  _const_spec(norm_g.shape), _const_spec(norm_b.shape),
                  _const_spec(w_s.shape), _const_spec(b_s.shape)],
        out_specs=[tok_spec, tok_spec, pl.BlockSpec((1, QKV_W, tm), lambda b, t: (b, 0, t)), tok_spec],
        out_shape=[out, out, out_t, out],
        compiler_params=pltpu.CompilerParams(
            dimension_semantics=("parallel", "parallel"), vmem_limit_bytes=VMEM_LIMIT),
        name="inproj",
    )(h, w_in, cos_t, sa_t, sb_t, norm_g, norm_b, w_s, b_s)


def _ctx_kv_kernel(h_ref, w_ref, k_ref, v_ref):
    h = h_ref[0]
    k_ref[0] = _dot(h, w_ref[:, QKV_W:2 * QKV_W]).astype(BF16)
    v_ref[0] = _dot(h, w_ref[:, 2 * QKV_W:3 * QKV_W]).T.astype(BF16)


def _ctx_kv(h, w_in):
    nb, nt, _ = h.shape
    blk = lambda b: (b, 0, 0)
    return pl.pallas_call(
        _ctx_kv_kernel,
        grid=(nb,),
        in_specs=[pl.BlockSpec((1, nt, D), blk), _const_spec(w_in.shape)],
        out_specs=[pl.BlockSpec((1, nt, QKV_W), blk), pl.BlockSpec((1, QKV_W, nt), blk)],
        out_shape=[jax.ShapeDtypeStruct((nb, nt, QKV_W), BF16), jax.ShapeDtypeStruct((nb, QKV_W, nt), BF16)],
        compiler_params=pltpu.CompilerParams(vmem_limit_bytes=VMEM_LIMIT),
        name="ctx_kv",
    )(h, w_in)


def _rope_tables(seq):
    rows = seq // GRID_W
    row = jnp.repeat(jnp.arange(rows), GRID_W)
    col = jnp.tile(jnp.arange(GRID_W), rows)
    inv = ROPE_BASE ** (-jnp.arange(ROPE_FREQS, dtype=F32) / ROPE_FREQS)
    ang = jnp.stack([row[:, None] * inv, col[:, None] * inv], axis=1)
    cos, sin = jnp.cos(ang), jnp.sin(ang)
    zero = jnp.zeros_like(sin)
    cos_c = jnp.stack([cos, cos], axis=2).reshape(seq, COMP)
    sa_c = jnp.stack([-sin, zero], axis=2).reshape(seq, COMP)
    sb_c = jnp.stack([zero, sin], axis=2).reshape(seq, COMP)
    two = lambda t: jnp.concatenate([t, t], axis=-1)
    return two(cos_c), two(sa_c), two(sb_c)


def _attn_kernel(q_ref, k_ref, vt_ref, kc_ref, vct_ref, lam_ref, g_ref, o_ref, acc_scr, *, lam_init):
    @pl.when(pl.program_id(0) == 0)
    def _():
        acc_scr[...] = jnp.ones_like(acc_scr)

    lv = lam_ref[...]
    lam = (jnp.exp(jnp.sum(lv[0:1] * lv[1:2], axis=-1, keepdims=True))
           - jnp.exp(jnp.sum(lv[2:3] * lv[3:4], axis=-1, keepdims=True)) + lam_init)
    tq = q_ref.shape[1]
    sq = Q_STREAM
    nt_dims = (((1,), (1,)), ((), ()))
    chunks = [(kr, vr, slice(k0, min(k0 + KEY_CHUNK, kr.shape[1])))
              for kr, vr in ((k_ref, vt_ref), (kc_ref, vct_ref)) for k0 in range(0, kr.shape[1], KEY_CHUNK)]
    lane = lax.broadcasted_iota(jnp.int32, (sq, HEAD_W), 1)

    qq = []
    for j in range(tq // sq):
        qh = q_ref[0, j * sq:(j + 1) * sq, :]
        zero = jnp.zeros_like(qh)
        qq.append(jnp.concatenate([jnp.where(lane < COMP, qh, zero),
                                   jnp.where(lane >= COMP, qh, zero)], axis=0))

    for j in range(tq // sq):
        prev = acc_scr[j]
        r = 1.0 / prev[HEAD_W:HEAD_W + 1]
        o_t = prev[:HEAD_W, :sq] * r[:, :sq] - prev[:HEAD_W, sq:] * (lam * r[:, sq:])
        inv = lax.rsqrt(jnp.mean(o_t * o_t, axis=0, keepdims=True) + EPS)
        y = (o_t * inv).T * (g_ref[...] * (1.0 - lam_init))
        o_ref[0, j * sq:(j + 1) * sq, :] = y.astype(BF16)

    def scores(chunk, j):
        kr, _, rows = chunk
        return lax.dot_general(kr[0, rows, :], qq[j], nt_dims, preferred_element_type=F32)

    n_streams = tq // sq
    m = [jnp.full((1, 2 * sq), -1e30, F32)] * n_streams
    acc = [jnp.zeros((HEAD_W + ONES_ROWS, 2 * sq), F32)] * n_streams
    s_cur = [scores(chunks[0], j) for j in range(n_streams)]
    for c, (_, vr, rows) in enumerate(chunks):
        ones = jnp.ones((ONES_ROWS, rows.stop - rows.start), BF16)
        vt = jnp.concatenate([vr[0, :, rows], ones], axis=0)
        for j in range(n_streams):
            s = s_cur[j]
            m_new = jnp.maximum(m[j], jnp.max(s, axis=0, keepdims=True))
            e = jnp.exp2((s - m_new).astype(BF16))
            acc[j] = acc[j] * jnp.exp2(m[j] - m_new) + _dot(vt, e)
            m[j] = m_new
            if c + 1 < len(chunks):
                s_cur[j] = scores(chunks[c + 1], j)
    for j in range(n_streams):
        acc_scr[j] = acc[j]


def _attn(q, k, vt, kc, vct, lam_vec, subln_g, lam_init):
    nb, nt, _ = q.shape
    nc = kc.shape[1]
    tq = ATTN_Q_TILE
    assert nt % KEY_CHUNK == 0 and nt % tq == 0
    n_t = nt // tq
    n_tiles = nb * HEADS * n_t

    def tile(i):
        return i // (HEADS * n_t), (i // n_t) % HEADS, i % n_t

    def cur(i):
        return tile(jnp.minimum(i, n_tiles - 1))

    def prev(i):
        return tile(jnp.maximum(i - 1, 0))

    keys = lambda n: pl.BlockSpec((1, n, HEAD_W), lambda i: (cur(i)[0], 0, cur(i)[1]))
    vals = lambda n: pl.BlockSpec((1, HEAD_W, n), lambda i: (cur(i)[0], cur(i)[1], 0))
    return pl.pallas_call(
        functools.partial(_attn_kernel, lam_init=lam_init),
        grid=(n_tiles + 1,),
        in_specs=[pl.BlockSpec((1, tq, HEAD_W), lambda i: (cur(i)[0], cur(i)[2], cur(i)[1])),
                  keys(nt), vals(nt), keys(nc), vals(nc),
                  _const_spec(lam_vec.shape), _const_spec((1, HEAD_W))],
        out_specs=pl.BlockSpec((1, tq, HEAD_W), lambda i: (prev(i)[0], prev(i)[2], prev(i)[1])),
        out_shape=jax.ShapeDtypeStruct((nb, nt, QKV_W), BF16),
        scratch_shapes=[pltpu.VMEM((tq // Q_STREAM, HEAD_W + ONES_ROWS, 2 * Q_STREAM), F32)],
        compiler_params=pltpu.CompilerParams(
            dimension_semantics=("arbitrary",), vmem_limit_bytes=VMEM_LIMIT),
        name="attn",
    )(q, k, vt, kc, vct, lam_vec, subln_g.reshape(1, HEAD_W))


def _fourier_consts():
    r, s, gc = DFT_R, SUB, FOURIER_GROUP_CH
    seq = r ** 3
    idx = np.arange(r)
    ang = 2.0 * np.pi * np.outer(idx, idx) / r
    c16, s16 = np.cos(ang), np.sin(ang)
    eye = np.eye(s)
    interleave = lambda m: m.reshape(2, r, s, -1).transpose(1, 0, 2, 3).reshape(2 * r * s, -1)
    ma = np.zeros((2, 2, s, s, r, s))
    for ka3 in range(2):
        for kal in range(s):
            for cl in range(s):
                ma[0, ka3, cl, kal, :, cl] = c16[8 * ka3 + kal]
                ma[1, ka3, cl, kal, :, cl] = -s16[8 * ka3 + kal]
    ma = interleave(ma.reshape(2 * r * s, r * s))
    kc_, ks_ = np.kron(c16, eye), np.kron(s16, eye)
    mb = interleave(np.block([[kc_, ks_], [-ks_, kc_]]))
    cc = np.zeros((r, s, s, 2, s))
    sc = np.zeros((r, s, s, 2, s))
    for cl in range(s):
        for c3 in range(2):
            for kal in range(s):
                cc[:, kal, cl, c3, kal] = c16[:, 8 * c3 + cl]
                sc[:, kal, cl, c3, kal] = s16[:, 8 * c3 + cl]
    cc, sc = cc.reshape(r * s, r * s), sc.reshape(r * s, r * s)
    mc = interleave(np.block([[cc, sc], [-sc, cc]]))
    def cdiag(theta):
        c, sn = np.diag(np.cos(theta)), np.diag(np.sin(theta))
        return np.block([[c, sn], [-sn, c]])

    mbg = np.zeros((2, s, 2, 2 * r * s, 2 * r * s))
    b_, kal_ = np.meshgrid(idx, np.arange(s), indexing="ij")
    for ka3 in range(2):
        for cl in range(s):
            for c3 in range(2):
                theta = 2.0 * np.pi * (16 * b_ + 8 * c3 + cl) * (8 * ka3 + kal_) / seq
                mbg[ka3, cl, c3] = mb @ cdiag(theta.reshape(-1))
    mbg = mbg.reshape(2 * r, 2 * r * s, 2 * r * s)
    mcg = np.zeros((r, 2 * r * s, 2 * r * s))
    cl_, c3_, kal_ = np.meshgrid(np.arange(s), np.arange(2), np.arange(s), indexing="ij")
    for kb in range(r):
        theta = 2.0 * np.pi * (8 * c3_ + cl_) * kb / (r * r) + 0.0 * kal_
        mcg[kb] = mc @ cdiag(theta.reshape(-1))
    ach = 2.0 * np.pi * np.outer(np.arange(gc), np.arange(gc)) / gc
    cd = np.concatenate([np.cos(ach), np.sin(ach)], axis=0) / math.sqrt(seq * gc)
    bf = lambda t: jnp.asarray(t, dtype=F32).astype(BF16)
    return bf(ma), bf(mbg), bf(mcg), bf(cd)


def _fourier_kernel(h_ref, ma_ref, mb_ref, mc_ref, cd_ref, o_ref, sa_scr, sb_scr):
    r, s, gc = DFT_R, SUB, FOURIER_GROUP_CH
    rows = r * s

    def split(y):
        y4 = y.reshape(r, 2, s, gc)
        return y4[:, 0].reshape(rows, gc), y4[:, 1].reshape(rows, gc)

    for b in range(r):
        for c3 in range(2):
            x = h_ref[0, :, 2 * b + c3, :, :].reshape(rows, gc).astype(BF16)
            re, im = split(_dot(ma_ref[...], x))
            sa_scr[0, :, :, b, c3] = re.reshape(2, s, s, gc)
            sa_scr[1, :, :, b, c3] = im.reshape(2, s, s, gc)
    for ka3 in range(2):
        for cl in range(s):
            for c3 in range(2):
                g = (ka3 * s + cl) * 2 + c3
                x = jnp.concatenate([sa_scr[0, ka3, cl, :, c3].reshape(rows, gc),
                                     sa_scr[1, ka3, cl, :, c3].reshape(rows, gc)], axis=0).astype(BF16)
                re, im = split(_dot(mb_ref[g], x))
                sb_scr[0, ka3, cl, :, c3] = re.reshape(r, s, gc)
                sb_scr[1, ka3, cl, :, c3] = im.reshape(r, s, gc)
    for ka3 in range(2):
        for kb in range(r):
            x = jnp.concatenate([sb_scr[0, ka3, :, kb, :].reshape(rows, gc),
                                 sb_scr[1, ka3, :, kb, :].reshape(rows, gc)], axis=0).astype(BF16)
            f = _dot(mc_ref[kb], x)
            fri = jnp.concatenate(split(f), axis=1).astype(BF16)
            o_ref[0, :, kb, ka3] = _dot(fri, cd_ref[...]).reshape(r, s, gc)


def _fourier(h, consts):
    nb, seq, _ = h.shape
    r, s, gc = DFT_R, SUB, FOURIER_GROUP_CH
    h5 = h.reshape(nb, r, 2 * r, s, D)
    stage_scratch = pltpu.VMEM((2, 2, s, r, 2, s, gc), F32)
    out = pl.pallas_call(
        _fourier_kernel,
        grid=(nb, D // gc),
        in_specs=[pl.BlockSpec((1, r, 2 * r, s, gc), lambda b, c: (b, 0, 0, 0, c))]
        + [_const_spec(a.shape) for a in consts],
        out_specs=pl.BlockSpec((1, r, r, 2, s, gc), lambda b, c: (b, 0, 0, 0, 0, c)),
        out_shape=jax.ShapeDtypeStruct((nb, r, r, 2, s, D), F32),
        scratch_shapes=[stage_scratch, stage_scratch],
        compiler_params=pltpu.CompilerParams(
            dimension_semantics=("parallel", "parallel"), vmem_limit_bytes=VMEM_LIMIT),
        name="fourier",
    )(h5, *consts)
    return out.reshape(nb, seq, D)


def kernel(x, c, ctx, c_ctx, ada_w, ada_b, ffn_w_gu, ffn_w_down, mix_w_in, mix_w_out, diff_lambda,
           diff_subln_g, sgu_norm_g, sgu_norm_b, sgu_w, sgu_b, fourier_w_out, fourier_b_out, final_norm_g):
    nb, seq, _ = x.shape
    assert seq == DFT_R ** 3 and DEPTH == 2
    ctx_row = nb
    cc = jnp.concatenate([c, c_ctx[None, :], jnp.zeros((2 * SUB - nb - 1, D), F32)], axis=0)
    mods = _ada(cc, ada_w, ada_b).reshape(DEPTH, cc.shape[0], N_MOD, D)
    latent = lambda b: b
    shared = lambda b: ctx_row

    ffn_w = (ffn_w_gu, ffn_w_down)

    x, h = _ffn(x, mods[0], latent, *ffn_w, (0, 0), mod_idx=(0, 1, 2), emit_idx=(3, 4))
    _, hc = _ffn(ctx.reshape(1, -1, D), mods[0], shared, *ffn_w, (0, 0), mod_idx=(0, 1, 2), emit_idx=(3, 4))
    hc = hc.reshape(ctx.shape)

    w_in = mix_w_in[0].astype(BF16)
    cos_t, sa_t, sb_t = _rope_tables(seq)
    bs_rows = jnp.repeat(sgu_b[0].T, GMLP_CHUNK, axis=1)
    q, k, vt, gated = _inproj(h, w_in, cos_t, sa_t, sb_t,
                             sgu_norm_g[0].reshape(1, GMLP_W), sgu_norm_b[0].reshape(1, GMLP_W),
                             sgu_w[0].astype(BF16), bs_rows)
    kc, vct = _ctx_kv(hc, w_in)
    lam_init = 0.8 - 0.6 * math.exp(-0.3 * 0)
    att = _attn(q, k, vt, kc, vct,
                diff_lambda[0], diff_subln_g[0], lam_init)
    w_out = mix_w_out[0].astype(BF16)
    x = _ffn(x, mods[0], latent, *ffn_w, (0, 1), mod_idx=(6, 7, 8),
             mixes=((att, w_out[:QKV_W]), (gated, w_out[QKV_W:])))

    x, h = _ffn(x, mods[1], latent, *ffn_w, (1, 0), mod_idx=(0, 1, 2), emit_idx=(3, 4), emit_dtype=F32)
    f = _fourier(h, _fourier_consts())
    return _ffn(x, mods[1], latent, *ffn_w, (1, 1), mod_idx=(6, 7, 8),
                mixes=((f, fourier_w_out[0].astype(BF16)),), mix_bias=fourier_b_out[0],
                final_g=final_norm_g)
```

```python
import functools
import math

import numpy as np
import jax
import jax.numpy as jnp
from jax import lax
from jax.experimental import pallas as pl
from jax.experimental.pallas import tpu as pltpu

D = 1024
DEPTH = 2
N_MOD = 9
D_FF = 2816
EPS = 1e-6
GRID_W = 64
HEADS = 4
HEAD_W = 128
COMP = 64
ROPE_FREQS = 16
ROPE_BASE = 10000.0
QKV_W = HEADS * HEAD_W
GMLP_W = 512
GMLP_CHUNK = 128
IN_COLS = 3 * QKV_W + 2 * GMLP_W
FOURIER_GROUP_CH = 256
DFT_R = 16
SUB = 8

VMEM_LIMIT = 56 * 1024 * 1024
FF_CHUNK = 256
TOKEN_TILE = 512
INPROJ_TILE = 1024
ATTN_Q_TILE = 2048
KEY_CHUNK = 256
ONES_ROWS = 16
Q_STREAM = 128
Q_SCALE = COMP ** -0.5 * math.log2(math.e)

BF16 = jnp.bfloat16
F32 = jnp.float32


def _const_spec(shape):
    zeros = (0,) * len(shape)
    return pl.BlockSpec(shape, lambda *_: zeros, pipeline_mode=pl.Buffered(1))


def _rms(x):
    return x * lax.rsqrt(jnp.mean(x * x, axis=-1, keepdims=True) + EPS)


def _dot(a, b):
    return jnp.dot(a, b, preferred_element_type=F32)


def _ada_kernel(c_ref, w_ref, b_ref, o_ref):
    s = jax.nn.silu(c_ref[...]).astype(BF16)
    o_ref[0] = _dot(s, w_ref[0].astype(BF16)) + b_ref[0]


def _ada(cc, ada_w, ada_b):
    rows = cc.shape[0]
    tn = 1024
    return pl.pallas_call(
        _ada_kernel,
        grid=(DEPTH, N_MOD * D // tn),
        in_specs=[
            pl.BlockSpec((rows, D), lambda l, j: (0, 0)),
            pl.BlockSpec((1, D, tn), lambda l, j: (l, 0, j)),
            pl.BlockSpec((1, 1, tn), lambda l, j: (l, 0, j)),
        ],
        out_specs=pl.BlockSpec((1, rows, tn), lambda l, j: (l, 0, j)),
        out_shape=jax.ShapeDtypeStruct((DEPTH, rows, N_MOD * D), F32),
        name="ada",
    )(cc, ada_w, ada_b.reshape(DEPTH, 1, N_MOD * D))


def _ffn_kernel(*refs, n_mix, mix_bias, mod_idx, emit_idx, final_norm):
    refs = list(refs)
    x_ref, mod_ref = refs[:2]
    pos = 2
    mix_refs = refs[pos:pos + 2 * n_mix]
    pos += 2 * n_mix
    bmix_ref = None
    if mix_bias:
        bmix_ref = refs[pos]
        pos += 1
    wgu_ref, wd_ref = refs[pos:pos + 2]
    pos += 2
    fng_ref = None
    if final_norm:
        fng_ref = refs[pos]
        pos += 1
    out_ref = refs[pos]
    pos += 1
    hn_ref = None
    if emit_idx is not None:
        hn_ref = refs[pos]

    def mod(i):
        return mod_ref[0, i:i + 1, :]

    x = x_ref[0]
    if n_mix:
        y = None
        for m in range(n_mix):
            t = _dot(mix_refs[2 * m][0].astype(BF16), mix_refs[2 * m + 1][...])
            y = t if y is None else y + t
        if mix_bias:
            y = y + bmix_ref[...]
        x = x + mod(5) * y

    sh, sc, gt = mod_idx
    h = (_rms(x) * (1.0 + mod(sc)) + mod(sh)).astype(BF16)
    acc = None
    for f in range(D_FF // FF_CHUNK):
        cols = slice(f * FF_CHUNK, (f + 1) * FF_CHUNK)
        up_cols = slice(D_FF + f * FF_CHUNK, D_FF + (f + 1) * FF_CHUNK)
        g = _dot(h, wgu_ref[0, 0, :, cols].astype(BF16))
        u = _dot(h, wgu_ref[0, 0, :, up_cols].astype(BF16))
        a = (jax.nn.silu(g) * u).astype(BF16)
        t = _dot(a, wd_ref[0, 0, cols, :].astype(BF16))
        acc = t if acc is None else acc + t

    out = x + (0.5 * mod(gt)) * acc
    if emit_idx is not None:
        esh, esc = emit_idx
        hn_ref[0] = (_rms(out) * (1.0 + mod(esc)) + mod(esh)).astype(hn_ref.dtype)
    if final_norm:
        out = _rms(out) * fng_ref[...]
    out_ref[0] = out


def _ffn(x, mod_l, mod_row, wgu, wd, w_idx, *, mod_idx, mixes=(), mix_bias=None,
         emit_idx=None, emit_dtype=BF16, final_g=None):
    nb, nt, _ = x.shape
    tm = min(TOKEN_TILE, nt)
    tok = lambda b, t: (b, t, 0)
    in_specs = [
        pl.BlockSpec((1, tm, D), tok),
        pl.BlockSpec((1, N_MOD, D), lambda b, t: (mod_row(b), 0, 0)),
    ]
    args = [x, mod_l]
    for arr, w in mixes:
        in_specs += [pl.BlockSpec((1, tm, arr.shape[-1]), tok), _const_spec(w.shape)]
        args += [arr, w]
    if mix_bias is not None:
        in_specs.append(_const_spec((1, D)))
        args.append(mix_bias.reshape(1, D))
    for w in (wgu, wd):
        in_specs.append(pl.BlockSpec((1, 1) + w.shape[2:], lambda b, t: w_idx + (0, 0),
                                     pipeline_mode=pl.Buffered(1)))
    args += [wgu, wd]
    if final_g is not None:
        in_specs.append(_const_spec((1, D)))
        args.append(final_g.reshape(1, D))
    out_specs = [pl.BlockSpec((1, tm, D), tok)]
    out_shape = [jax.ShapeDtypeStruct(x.shape, F32)]
    if emit_idx is not None:
        out_specs.append(pl.BlockSpec((1, tm, D), tok))
        out_shape.append(jax.ShapeDtypeStruct(x.shape, emit_dtype))
    kern = functools.partial(
        _ffn_kernel, n_mix=len(mixes), mix_bias=mix_bias is not None, mod_idx=mod_idx,
        emit_idx=emit_idx, final_norm=final_g is not None)
    res = pl.pallas_call(
        kern,
        grid=(nb, nt // tm),
        in_specs=in_specs,
        out_specs=out_specs,
        out_shape=out_shape,
        compiler_params=pltpu.CompilerParams(
            dimension_semantics=("parallel", "parallel"), vmem_limit_bytes=VMEM_LIMIT),
        name="ffn",
    )(*args)
    return res if emit_idx is not None else res[0]


def _inproj_kernel(h_ref, w_ref, cos_ref, sa_ref, sb_ref, ng_ref, nb_ref, ws_ref, bs_ref,
                   q_ref, k_ref, v_ref, g_ref):
    h = h_ref[0]
    tm = h.shape[0]

    def rope(x, scale):
        up = pltpu.roll(x, HEAD_W - ROPE_FREQS, 1)
        dn = pltpu.roll(x, ROPE_FREQS, 1)
        r = x * cos_ref[...] + up * sa_ref[...] + dn * sb_ref[...]
        return r * scale if scale != 1.0 else r

    def project_rope(ref, col0, scale):
        p = _dot(h, w_ref[:, col0:col0 + QKV_W])
        for hd in range(HEADS):
            sl = slice(hd * HEAD_W, (hd + 1) * HEAD_W)
            ref[0, :, sl] = rope(p[:, sl], scale).astype(BF16)

    z0 = 3 * QKV_W
    vg = jax.nn.gelu(_dot(h, w_ref[:, z0 + GMLP_W:]), approximate=True)
    u = jax.nn.gelu(_dot(h, w_ref[:, z0:z0 + GMLP_W]), approximate=True)
    tn = []
    for grp in range(GMLP_W // GMLP_CHUNK):
        sl = slice(grp * GMLP_CHUNK, (grp + 1) * GMLP_CHUNK)
        t = vg[:, sl]
        mu = jnp.mean(t, axis=-1, keepdims=True)
        tc = t - mu
        var = jnp.mean(tc * tc, axis=-1, keepdims=True)
        tn.append((tc * lax.rsqrt(var + EPS) * ng_ref[:, sl] + nb_ref[:, sl]).astype(BF16))
    project_rope(q_ref, 0, Q_SCALE)
    for grp in range(GMLP_W // GMLP_CHUNK):
        sl = slice(grp * GMLP_CHUNK, (grp + 1) * GMLP_CHUNK)
        for ch in range(tm // GMLP_CHUNK):
            rows = slice(ch * GMLP_CHUNK, (ch + 1) * GMLP_CHUNK)
            mixed = _dot(ws_ref[grp], tn[grp][rows]) + bs_ref[:, sl]
            g_ref[0, rows, sl] = (u[rows, sl] * mixed).astype(BF16)
    project_rope(k_ref, QKV_W, 1.0)
    v_ref[0] = _dot(h, w_ref[:, 2 * QKV_W:3 * QKV_W]).T.astype(BF16)


def _inproj(h, w_in, cos_t, sa_t, sb_t, norm_g, norm_b, w_s, b_s):
    nb, nt, _ = h.shape
    tm = INPROJ_TILE
    tok = lambda b, t: (b, t, 0)
    tab = pl.BlockSpec((tm, HEAD_W), lambda b, t: (t, 0))
    out = jax.ShapeDtypeStruct((nb, nt, QKV_W), BF16)
    out_t = jax.ShapeDtypeStruct((nb, QKV_W, nt), BF16)
    tok_spec = pl.BlockSpec((1, tm, QKV_W), tok)
    return pl.pallas_call(
        _inproj_kernel,
        grid=(nb, nt // tm),
        in_specs=[pl.BlockSpec((1, tm, D), tok), _const_spec(w_in.shape), tab, tab, tab,
                  _const_spec(norm_g.shape), _const_spec(norm_b.shape),
                  _const_spec(w_s.shape), _const_spec(b_s.shape)],
        out_specs=[tok_spec, tok_spec, pl.BlockSpec((1, QKV_W, tm), lambda b, t: (b, 0, t)), tok_spec],
        out_shape=[out, out, out_t, out],
        compiler_params=pltpu.CompilerParams(
            dimension_semantics=("parallel", "parallel"), vmem_limit_bytes=VMEM_LIMIT),
        name="inproj",
    )(h, w_in, cos_t, sa_t, sb_t, norm_g, norm_b, w_s, b_s)


def _ctx_kv_kernel(h_ref, w_ref, k_ref, v_ref):
    h = h_ref[0]
    k_ref[0] = _dot(h, w_ref[:, QKV_W:2 * QKV_W]).astype(BF16)
    v_ref[0] = _dot(h, w_ref[:, 2 * QKV_W:3 * QKV_W]).T.astype(BF16)


def _ctx_kv(h, w_in):
    nb, nt, _ = h.shape
    blk = lambda b: (b, 0, 0)
    return pl.pallas_call(
        _ctx_kv_kernel,
        grid=(nb,),
        in_specs=[pl.BlockSpec((1, nt, D), blk), _const_spec(w_in.shape)],
        out_specs=[pl.BlockSpec((1, nt, QKV_W), blk), pl.BlockSpec((1, QKV_W, nt), blk)],
        out_shape=[jax.ShapeDtypeStruct((nb, nt, QKV_W), BF16), jax.ShapeDtypeStruct((nb, QKV_W, nt), BF16)],
        compiler_params=pltpu.CompilerParams(vmem_limit_bytes=VMEM_LIMIT),
        name="ctx_kv",
    )(h, w_in)


def _rope_tables(seq):
    rows = seq // GRID_W
    row = jnp.repeat(jnp.arange(rows), GRID_W)
    col = jnp.tile(jnp.arange(GRID_W), rows)
    inv = ROPE_BASE ** (-jnp.arange(ROPE_FREQS, dtype=F32) / ROPE_FREQS)
    ang = jnp.stack([row[:, None] * inv, col[:, None] * inv], axis=1)
    cos, sin = jnp.cos(ang), jnp.sin(ang)
    zero = jnp.zeros_like(sin)
    cos_c = jnp.stack([cos, cos], axis=2).reshape(seq, COMP)
    sa_c = jnp.stack([-sin, zero], axis=2).reshape(seq, COMP)
    sb_c = jnp.stack([zero, sin], axis=2).reshape(seq, COMP)
    two = lambda t: jnp.concatenate([t, t], axis=-1)
    return two(cos_c), two(sa_c), two(sb_c)


def _attn_kernel(q_ref, k_ref, vt_ref, kc_ref, vct_ref, lam_ref, g_ref, o_ref, acc_scr, *, lam_init):
    @pl.when(pl.program_id(0) == 0)
    def _():
        acc_scr[...] = jnp.ones_like(acc_scr)

    lv = lam_ref[...]
    lam = (jnp.exp(jnp.sum(lv[0:1] * lv[1:2], axis=-1, keepdims=True))
           - jnp.exp(jnp.sum(lv[2:3] * lv[3:4], axis=-1, keepdims=True)) + lam_init)
    tq = q_ref.shape[1]
    sq = Q_STREAM
    nt_dims = (((1,), (1,)), ((), ()))
    chunks = [(kr, vr, slice(k0, min(k0 + KEY_CHUNK, kr.shape[1])))
              for kr, vr in ((k_ref, vt_ref), (kc_ref, vct_ref)) for k0 in range(0, kr.shape[1], KEY_CHUNK)]
    lane = lax.broadcasted_iota(jnp.int32, (sq, HEAD_W), 1)

    qq = []
    for j in range(tq // sq):
        qh = q_ref[0, j * sq:(j + 1) * sq, :]
        zero = jnp.zeros_like(qh)
        qq.append(jnp.concatenate([jnp.where(lane < COMP, qh, zero),
                                   jnp.where(lane >= COMP, qh, zero)], axis=0))

    for j in range(tq // sq):
        prev = acc_scr[j]
        r = 1.0 / prev[HEAD_W:HEAD_W + 1]
        o_t = prev[:HEAD_W, :sq] * r[:, :sq] - prev[:HEAD_W, sq:] * (lam * r[:, sq:])
        inv = lax.rsqrt(jnp.mean(o_t * o_t, axis=0, keepdims=True) + EPS)
        y = (o_t * inv).T * (g_ref[...] * (1.0 - lam_init))
        o_ref[0, j * sq:(j + 1) * sq, :] = y.astype(BF16)

    def scores(chunk, j):
        kr, _, rows = chunk
        return lax.dot_general(kr[0, rows, :], qq[j], nt_dims, preferred_element_type=F32)

    n_streams = tq // sq
    m = [jnp.full((1, 2 * sq), -1e30, F32)] * n_streams
    acc = [jnp.zeros((HEAD_W + ONES_ROWS, 2 * sq), F32)] * n_streams
    s_cur = [scores(chunks[0], j) for j in range(n_streams)]
    for c, (_, vr, rows) in enumerate(chunks):
        ones = jnp.ones((ONES_ROWS, rows.stop - rows.start), BF16)
        vt = jnp.concatenate([vr[0, :, rows], ones], axis=0)
        for j in range(n_streams):
            s = s_cur[j]
            m_new = jnp.maximum(m[j], jnp.max(s, axis=0, keepdims=True))
            e = jnp.exp2((s - m_new).astype(BF16))
            acc[j] = acc[j] * jnp.exp2(m[j] - m_new) + _dot(vt, e)
            m[j] = m_new
            if c + 1 < len(chunks):
                s_cur[j] = scores(chunks[c + 1], j)
    for j in range(n_streams):
        acc_scr[j] = acc[j]


def _attn(q, k, vt, kc, vct, lam_vec, subln_g, lam_init):
    nb, nt, _ = q.shape
    nc = kc.shape[1]
    tq = ATTN_Q_TILE
    assert nt % KEY_CHUNK == 0 and nt % tq == 0
    n_t = nt // tq
    n_tiles = nb * HEADS * n_t

    def tile(i):
        return i // (HEADS * n_t), (i // n_t) % HEADS, i % n_t

    def cur(i):
        return tile(jnp.minimum(i, n_tiles - 1))

    def prev(i):
        return tile(jnp.maximum(i - 1, 0))

    keys = lambda n: pl.BlockSpec((1, n, HEAD_W), lambda i: (cur(i)[0], 0, cur(i)[1]))
    vals = lambda n: pl.BlockSpec((1, HEAD_W, n), lambda i: (cur(i)[0], cur(i)[1], 0))
    return pl.pallas_call(
        functools.partial(_attn_kernel, lam_init=lam_init),
        grid=(n_tiles + 1,),
        in_specs=[pl.BlockSpec((1, tq, HEAD_W), lambda i: (cur(i)[0], cur(i)[2], cur(i)[1])),
                  keys(nt), vals(nt), keys(nc), vals(nc),
                  _const_spec(lam_vec.shape), _const_spec((1, HEAD_W))],
        out_specs=pl.BlockSpec((1, tq, HEAD_W), lambda i: (prev(i)[0], prev(i)[2], prev(i)[1])),
        out_shape=jax.ShapeDtypeStruct((nb, nt, QKV_W), BF16),
        scratch_shapes=[pltpu.VMEM((tq // Q_STREAM, HEAD_W + ONES_ROWS, 2 * Q_STREAM), F32)],
        compiler_params=pltpu.CompilerParams(
            dimension_semantics=("arbitrary",), vmem_limit_bytes=VMEM_LIMIT),
        name="attn",
    )(q, k, vt, kc, vct, lam_vec, subln_g.reshape(1, HEAD_W))


def _fourier_consts():
    r, s, gc = DFT_R, SUB, FOURIER_GROUP_CH
    seq = r ** 3
    idx = np.arange(r)
    ang = 2.0 * np.pi * np.outer(idx, idx) / r
    c16, s16 = np.cos(ang), np.sin(ang)
    eye = np.eye(s)
    interleave = lambda m: m.reshape(2, r, s, -1).transpose(1, 0, 2, 3).reshape(2 * r * s, -1)
    ma = np.zeros((2, 2, s, s, r, s))
    for ka3 in range(2):
        for kal in range(s):
            for cl in range(s):
                ma[0, ka3, cl, kal, :, cl] = c16[8 * ka3 + kal]
                ma[1, ka3, cl, kal, :, cl] = -s16[8 * ka3 + kal]
    ma = interleave(ma.reshape(2 * r * s, r * s))
    kc_, ks_ = np.kron(c16, eye), np.kron(s16, eye)
    mb = interleave(np.block([[kc_, ks_], [-ks_, kc_]]))
    cc = np.zeros((r, s, s, 2, s))
    sc = np.zeros((r, s, s, 2, s))
    for cl in range(s):
        for c3 in range(2):
            for kal in range(s):
                cc[:, kal, cl, c3, kal] = c16[:, 8 * c3 + cl]
                sc[:, kal, cl, c3, kal] = s16[:, 8 * c3 + cl]
    cc, sc = cc.reshape(r * s, r * s), sc.reshape(r * s, r * s)
    mc = interleave(np.block([[cc, sc], [-sc, cc]]))
    def cdiag(theta):
        c, sn = np.diag(np.cos(theta)), np.diag(np.sin(theta))
        return np.block([[c, sn], [-sn, c]])

    mbg = np.zeros((2, s, 2, 2 * r * s, 2 * r * s))
    b_, kal_ = np.meshgrid(idx, np.arange(s), indexing="ij")
    for ka3 in range(2):
        for cl in range(s):
            for c3 in range(2):
                theta = 2.0 * np.pi * (16 * b_ + 8 * c3 + cl) * (8 * ka3 + kal_) / seq
                mbg[ka3, cl, c3] = mb @ cdiag(theta.reshape(-1))
    mbg = mbg.reshape(2 * r, 2 * r * s, 2 * r * s)
    mcg = np.zeros((r, 2 * r * s, 2 * r * s))
    cl_, c3_, kal_ = np.meshgrid(np.arange(s), np.arange(2), np.arange(s), indexing="ij")
    for kb in range(r):
        theta = 2.0 * np.pi * (8 * c3_ + cl_) * kb / (r * r) + 0.0 * kal_
        mcg[kb] = mc @ cdiag(theta.reshape(-1))
    ach = 2.0 * np.pi * np.outer(np.arange(gc), np.arange(gc)) / gc
    cd = np.concatenate([np.cos(ach), np.sin(ach)], axis=0) / math.sqrt(seq * gc)
    bf = lambda t: jnp.asarray(t, dtype=F32).astype(BF16)
    return bf(ma), bf(mbg), bf(mcg), bf(cd)


def _fourier_kernel(h_ref, ma_ref, mb_ref, mc_ref, cd_ref, o_ref, sa_scr, sb_scr):
    r, s, gc = DFT_R, SUB, FOURIER_GROUP_CH
    rows = r * s

    def split(y):
        y4 = y.reshape(r, 2, s, gc)
        return y4[:, 0].reshape(rows, gc), y4[:, 1].reshape(rows, gc)

    for b in range(r):
        for c3 in range(2):
            x = h_ref[0, :, 2 * b + c3, :, :].reshape(rows, gc).astype(BF16)
            re, im = split(_dot(ma_ref[...], x))
            sa_scr[0, :, :, b, c3] = re.reshape(2, s, s, gc)
            sa_scr[1, :, :, b, c3] = im.reshape(2, s, s, gc)
    for ka3 in range(2):
        for cl in range(s):
            for c3 in range(2):
                g = (ka3 * s + cl) * 2 + c3
                x = jnp.concatenate([sa_scr[0, ka3, cl, :, c3].reshape(rows, gc),
                                     sa_scr[1, ka3, cl, :, c3].reshape(rows, gc)], axis=0).astype(BF16)
                re, im = split(_dot(mb_ref[g], x))
                sb_scr[0, ka3, cl, :, c3] = re.reshape(r, s, gc)
                sb_scr[1, ka3, cl, :, c3] = im.reshape(r, s, gc)
    for ka3 in range(2):
        for kb in range(r):
            x = jnp.concatenate([sb_scr[0, ka3, :, kb, :].reshape(rows, gc),
                                 sb_scr[1, ka3, :, kb, :].reshape(rows, gc)], axis=0).astype(BF16)
            f = _dot(mc_ref[kb], x)
            fri = jnp.concatenate(split(f), axis=1).astype(BF16)
            o_ref[0, :, kb, ka3] = _dot(fri, cd_ref[...]).reshape(r, s, gc)


def _fourier(h, consts):
    nb, seq, _ = h.shape
    r, s, gc = DFT_R, SUB, FOURIER_GROUP_CH
    h5 = h.reshape(nb, r, 2 * r, s, D)
    stage_scratch = pltpu.VMEM((2, 2, s, r, 2, s, gc), F32)
    out = pl.pallas_call(
        _fourier_kernel,
        grid=(nb, D // gc),
        in_specs=[pl.BlockSpec((1, r, 2 * r, s, gc), lambda b, c: (b, 0, 0, 0, c))]
        + [_const_spec(a.shape) for a in consts],
        out_specs=pl.BlockSpec((1, r, r, 2, s, gc), lambda b, c: (b, 0, 0, 0, 0, c)),
        out_shape=jax.ShapeDtypeStruct((nb, r, r, 2, s, D), F32),
        scratch_shapes=[stage_scratch, stage_scratch],
        compiler_params=pltpu.CompilerParams(
            dimension_semantics=("parallel", "parallel"), vmem_limit_bytes=VMEM_LIMIT),
        name="fourier",
    )(h5, *consts)
    return out.reshape(nb, seq, D)


def kernel(x, c, ctx, c_ctx, ada_w, ada_b, ffn_w_gu, ffn_w_down, mix_w_in, mix_w_out, diff_lambda,
           diff_subln_g, sgu_norm_g, sgu_norm_b, sgu_w, sgu_b, fourier_w_out, fourier_b_out, final_norm_g):
    nb, seq, _ = x.shape
    assert seq == DFT_R ** 3 and DEPTH == 2
    ctx_row = nb
    cc = jnp.concatenate([c, c_ctx[None, :], jnp.zeros((2 * SUB - nb - 1, D), F32)], axis=0)
    mods = _ada(cc, ada_w, ada_b).reshape(DEPTH, cc.shape[0], N_MOD, D)
    latent = lambda b: b
    shared = lambda b: ctx_row

    ffn_w = (ffn_w_gu, ffn_w_down)

    x, h = _ffn(x, mods[0], latent, *ffn_w, (0, 0), mod_idx=(0, 1, 2), emit_idx=(3, 4))
    _, hc = _ffn(ctx.reshape(1, -1, D), mods[0], shared, *ffn_w, (0, 0), mod_idx=(0, 1, 2), emit_idx=(3, 4))
    hc = hc.reshape(ctx.shape)

    w_in = mix_w_in[0].astype(BF16)
    cos_t, sa_t, sb_t = _rope_tables(seq)
    bs_rows = jnp.repeat(sgu_b[0].T, GMLP_CHUNK, axis=1)
    q, k, vt, gated = _inproj(h, w_in, cos_t, sa_t, sb_t,
                             sgu_norm_g[0].reshape(1, GMLP_W), sgu_norm_b[0].reshape(1, GMLP_W),
                             sgu_w[0].astype(BF16), bs_rows)
    kc, vct = _ctx_kv(hc, w_in)
    lam_init = 0.8 - 0.6 * math.exp(-0.3 * 0)
    att = _attn(q, k, vt, kc, vct,
                diff_lambda[0], diff_subln_g[0], lam_init)
    w_out = mix_w_out[0].astype(BF16)
    x = _ffn(x, mods[0], latent, *ffn_w, (0, 1), mod_idx=(6, 7, 8),
             mixes=((att, w_out[:QKV_W]), (gated, w_out[QKV_W:])))

    x, h = _ffn(x, mods[1], latent, *ffn_w, (1, 0), mod_idx=(0, 1, 2), emit_idx=(3, 4), emit_dtype=F32)
    f = _fourier(h, _fourier_consts())
    return _ffn(x, mods[1], latent, *ffn_w, (1, 1), mod_idx=(6, 7, 8),
                mixes=((f, fourier_w_out[0].astype(BF16)),), mix_bias=fourier_b_out[0],
                final_g=final_norm_g)
```

```python
import functools
import math

import numpy as np
import jax
import jax.numpy as jnp
from jax import lax
from jax.experimental import pallas as pl
from jax.experimental.pallas import tpu as pltpu

D = 1024
DEPTH = 2
N_MOD = 9
D_FF = 2816
EPS = 1e-6
GRID_W = 64
HEADS = 4
HEAD_W = 128
COMP = 64
ROPE_FREQS = 16
ROPE_BASE = 10000.0
QKV_W = HEADS * HEAD_W
GMLP_W = 512
GMLP_CHUNK = 128
IN_COLS = 3 * QKV_W + 2 * GMLP_W
FOURIER_GROUP_CH = 256
DFT_R = 16
SUB = 8

VMEM_LIMIT = 56 * 1024 * 1024
FF_CHUNK = 256
TOKEN_TILE = 512
INPROJ_TILE = 1024
ATTN_Q_TILE = 1024
KEY_CHUNK = 256
ONES_ROWS = 16
Q_STREAM = 128
Q_SCALE = COMP ** -0.5 * math.log2(math.e)

BF16 = jnp.bfloat16
F32 = jnp.float32


def _const_spec(shape):
    zeros = (0,) * len(shape)
    return pl.BlockSpec(shape, lambda *_: zeros, pipeline_mode=pl.Buffered(1))


def _rms(x):
    return x * lax.rsqrt(jnp.mean(x * x, axis=-1, keepdims=True) + EPS)


def _dot(a, b):
    return jnp.dot(a, b, preferred_element_type=F32)


def _ada_kernel(c_ref, w_ref, b_ref, o_ref):
    s = jax.nn.silu(c_ref[...]).astype(BF16)
    o_ref[0] = _dot(s, w_ref[0].astype(BF16)) + b_ref[0]


def _ada(cc, ada_w, ada_b):
    rows = cc.shape[0]
    tn = 1024
    return pl.pallas_call(
        _ada_kernel,
        grid=(DEPTH, N_MOD * D // tn),
        in_specs=[
            pl.BlockSpec((rows, D), lambda l, j: (0, 0)),
            pl.BlockSpec((1, D, tn), lambda l, j: (l, 0, j)),
            pl.BlockSpec((1, 1, tn), lambda l, j: (l, 0, j)),
        ],
        out_specs=pl.BlockSpec((1, rows, tn), lambda l, j: (l, 0, j)),
        out_shape=jax.ShapeDtypeStruct((DEPTH, rows, N_MOD * D), F32),
        name="ada",
    )(cc, ada_w, ada_b.reshape(DEPTH, 1, N_MOD * D))


def _ffn_kernel(*refs, n_mix, mix_bias, mod_idx, emit_idx, final_norm):
    refs = list(refs)
    x_ref, mod_ref = refs[:2]
    pos = 2
    mix_refs = refs[pos:pos + 2 * n_mix]
    pos += 2 * n_mix
    bmix_ref = None
    if mix_bias:
        bmix_ref = refs[pos]
        pos += 1
    wgu_ref, wd_ref = refs[pos:pos + 2]
    pos += 2
    fng_ref = None
    if final_norm:
        fng_ref = refs[pos]
        pos += 1
    out_ref = refs[pos]
    pos += 1
    hn_ref = None
    if emit_idx is not None:
        hn_ref = refs[pos]

    def mod(i):
        return mod_ref[0, i:i + 1, :]

    sh, sc, gt = mod_idx
    tm = x_ref.shape[1]
    halves = (slice(0, tm // 2), slice(tm // 2, tm))
    n_chunks = D_FF // FF_CHUNK

    def swiglu_chunk(h, f):
        cols = slice(f * FF_CHUNK, (f + 1) * FF_CHUNK)
        up_cols = slice(D_FF + f * FF_CHUNK, D_FF + (f + 1) * FF_CHUNK)
        g = _dot(h, wgu_ref[0, 0, :, cols].astype(BF16))
        u = _dot(h, wgu_ref[0, 0, :, up_cols].astype(BF16))
        a = (jax.nn.silu(g) * u).astype(BF16)
        return _dot(a, wd_ref[0, 0, cols, :].astype(BF16))

    xs, hs = [], []
    for rows in halves:
        x = x_ref[0, rows, :]
        if n_mix:
            y = None
            for m in range(n_mix):
                t = _dot(mix_refs[2 * m][0, rows, :].astype(BF16), mix_refs[2 * m + 1][...])
                y = t if y is None else y + t
            if mix_bias:
                y = y + bmix_ref[...]
            x = x + mod(5) * y
        xs.append(x)
        hs.append((_rms(x) * (1.0 + mod(sc)) + mod(sh)).astype(BF16))
    acc = [swiglu_chunk(hs[r], 0) for r in range(2)]
    h_all = jnp.concatenate(hs, axis=0)
    for f in range(1, n_chunks - 1):
        t = swiglu_chunk(h_all, f)
        acc = [acc[r] + t[halves[r]] for r in range(2)]
    for r, rows in enumerate(halves):
        out = xs[r] + (0.5 * mod(gt)) * (acc[r] + swiglu_chunk(hs[r], n_chunks - 1))
        if emit_idx is not None:
            esh, esc = emit_idx
            hn_ref[0, rows, :] = (_rms(out) * (1.0 + mod(esc)) + mod(esh)).astype(hn_ref.dtype)
        if final_norm:
            out = _rms(out) * fng_ref[...]
        out_ref[0, rows, :] = out


def _ffn(x, mod_l, mod_row, wgu, wd, w_idx, *, mod_idx, mixes=(), mix_bias=None,
         emit_idx=None, emit_dtype=BF16, final_g=None):
    nb, nt, _ = x.shape
    tm = min(TOKEN_TILE, nt)
    tok = lambda b, t: (b, t, 0)
    in_specs = [
        pl.BlockSpec((1, tm, D), tok),
        pl.BlockSpec((1, N_MOD, D), lambda b, t: (mod_row(b), 0, 0)),
    ]
    args = [x, mod_l]
    for arr, w in mixes:
        in_specs += [pl.BlockSpec((1, tm, arr.shape[-1]), tok), _const_spec(w.shape)]
        args += [arr, w]
    if mix_bias is not None:
        in_specs.append(_const_spec((1, D)))
        args.append(mix_bias.reshape(1, D))
    for w in (wgu, wd):
        in_specs.append(pl.BlockSpec((1, 1) + w.shape[2:], lambda b, t: w_idx + (0, 0),
                                     pipeline_mode=pl.Buffered(1)))
    args += [wgu, wd]
    if final_g is not None:
        in_specs.append(_const_spec((1, D)))
        args.append(final_g.reshape(1, D))
    out_specs = [pl.BlockSpec((1, tm, D), tok)]
    out_shape = [jax.ShapeDtypeStruct(x.shape, F32)]
    if emit_idx is not None:
        out_specs.append(pl.BlockSpec((1, tm, D), tok))
        out_shape.append(jax.ShapeDtypeStruct(x.shape, emit_dtype))
    kern = functools.partial(
        _ffn_kernel, n_mix=len(mixes), mix_bias=mix_bias is not None, mod_idx=mod_idx,
        emit_idx=emit_idx, final_norm=final_g is not None)
    res = pl.pallas_call(
        kern,
        grid=(nb, nt // tm),
        in_specs=in_specs,
        out_specs=out_specs,
        out_shape=out_shape,
        compiler_params=pltpu.CompilerParams(
            dimension_semantics=("parallel", "parallel"), vmem_limit_bytes=VMEM_LIMIT),
        name="ffn",
    )(*args)
    return res if emit_idx is not None else res[0]


def _inproj_kernel(h_ref, w_ref, cos_ref, sa_ref, sb_ref, ng_ref, nb_ref, ws_ref, bs_ref,
                   q_ref, k_ref, v_ref, g_ref):
    h = h_ref[0]
    tm = h.shape[0]

    def rope(x, scale):
        up = pltpu.roll(x, HEAD_W - ROPE_FREQS, 1)
        dn = pltpu.roll(x, ROPE_FREQS, 1)
        r = x * cos_ref[...] + up * sa_ref[...] + dn * sb_ref[...]
        return r * scale if scale != 1.0 else r

    def project_rope(ref, col0, scale):
        p = _dot(h, w_ref[:, col0:col0 + QKV_W])
        for hd in range(HEADS):
            sl = slice(hd * HEAD_W, (hd + 1) * HEAD_W)
            ref[0, :, sl] = rope(p[:, sl], scale).astype(BF16)

    z0 = 3 * QKV_W
    vg = jax.nn.gelu(_dot(h, w_ref[:, z0 + GMLP_W:]), approximate=True)
    u = jax.nn.gelu(_dot(h, w_ref[:, z0:z0 + GMLP_W]), approximate=True)
    tn = []
    for grp in range(GMLP_W // GMLP_CHUNK):
        sl = slice(grp * GMLP_CHUNK, (grp + 1) * GMLP_CHUNK)
        t = vg[:, sl]
        mu = jnp.mean(t, axis=-1, keepdims=True)
        tc = t - mu
        var = jnp.mean(tc * tc, axis=-1, keepdims=True)
        tn.append((tc * lax.rsqrt(var + EPS) * ng_ref[:, sl] + nb_ref[:, sl]).astype(BF16))
    project_rope(q_ref, 0, Q_SCALE)
    for grp in range(GMLP_W // GMLP_CHUNK):
        sl = slice(grp * GMLP_CHUNK, (grp + 1) * GMLP_CHUNK)
        for ch in range(tm // GMLP_CHUNK):
            rows = slice(ch * GMLP_CHUNK, (ch + 1) * GMLP_CHUNK)
            mixed = _dot(ws_ref[grp], tn[grp][rows]) + bs_ref[:, sl]
            g_ref[0, rows, sl] = (u[rows, sl] * mixed).astype(BF16)
    project_rope(k_ref, QKV_W, 1.0)
    v_ref[0] = _dot(h, w_ref[:, 2 * QKV_W:3 * QKV_W]).T.astype(BF16)


def _inproj(h, w_in, cos_t, sa_t, sb_t, norm_g, norm_b, w_s, b_s):
    nb, nt, _ = h.shape
    tm = INPROJ_TILE
    tok = lambda b, t: (b, t, 0)
    tab = pl.BlockSpec((tm, HEAD_W), lambda b, t: (t, 0))
    out = jax.ShapeDtypeStruct((nb, nt, QKV_W), BF16)
    out_t = jax.ShapeDtypeStruct((nb, QKV_W, nt), BF16)
    tok_spec = pl.BlockSpec((1, tm, QKV_W), tok)
    return pl.pallas_call(
        _inproj_kernel,
        grid=(nb, nt // tm),
        in_specs=[pl.BlockSpec((1, tm, D), tok), _const_spec(w_in.shape), tab, tab, tab,
                  _const_spec(norm_g.shape), _const_spec(norm_b.shape),
                  _const_spec(w_s.shape), _const_spec(b_s.shape)],
        out_specs=[tok_spec, tok_spec, pl.BlockSpec((1, QKV_W, tm), lambda b, t: (b, 0, t)), tok_spec],
        out_shape=[out, out, out_t, out],
        compiler_params=pltpu.CompilerParams(
            dimension_semantics=("parallel", "parallel"), vmem_limit_bytes=VMEM_LIMIT),
        name="inproj",
    )(h, w_in, cos_t, sa_t, sb_t, norm_g, norm_b, w_s, b_s)


def _ctx_kv_kernel(h_ref, w_ref, k_ref, v_ref):
    h = h_ref[0]
    k_ref[0] = _dot(h, w_ref[:, QKV_W:2 * QKV_W]).astype(BF16)
    v_ref[0] = _dot(h, w_ref[:, 2 * QKV_W:3 * QKV_W]).T.astype(BF16)


def _ctx_kv(h, w_in):
    nb, nt, _ = h.shape
    blk = lambda b: (b, 0, 0)
    return pl.pallas_call(
        _ctx_kv_kernel,
        grid=(nb,),
        in_specs=[pl.BlockSpec((1, nt, D), blk), _const_spec(w_in.shape)],
        out_specs=[pl.BlockSpec((1, nt, QKV_W), blk), pl.BlockSpec((1, QKV_W, nt), blk)],
        out_shape=[jax.ShapeDtypeStruct((nb, nt, QKV_W), BF16), jax.ShapeDtypeStruct((nb, QKV_W, nt), BF16)],
        compiler_params=pltpu.CompilerParams(vmem_limit_bytes=VMEM_LIMIT),
        name="ctx_kv",
    )(h, w_in)


def _rope_tables(seq):
    rows = seq // GRID_W
    row = jnp.repeat(jnp.arange(rows), GRID_W)
    col = jnp.tile(jnp.arange(GRID_W), rows)
    inv = ROPE_BASE ** (-jnp.arange(ROPE_FREQS, dtype=F32) / ROPE_FREQS)
    ang = jnp.stack([row[:, None] * inv, col[:, None] * inv], axis=1)
    cos, sin = jnp.cos(ang), jnp.sin(ang)
    zero = jnp.zeros_like(sin)
    cos_c = jnp.stack([cos, cos], axis=2).reshape(seq, COMP)
    sa_c = jnp.stack([-sin, zero], axis=2).reshape(seq, COMP)
    sb_c = jnp.stack([zero, sin], axis=2).reshape(seq, COMP)
    two = lambda t: jnp.concatenate([t, t], axis=-1)
    return two(cos_c), two(sa_c), two(sb_c)


def _attn_kernel(q_ref, k_ref, vt_ref, kc_ref, vct_ref, lam_ref, g_ref, o_ref, acc_scr, *, lam_init):
    @pl.when(pl.program_id(0) == 0)
    def _():
        acc_scr[...] = jnp.ones_like(acc_scr)

    lv = lam_ref[...]
    lam = (jnp.exp(jnp.sum(lv[0:1] * lv[1:2], axis=-1, keepdims=True))
           - jnp.exp(jnp.sum(lv[2:3] * lv[3:4], axis=-1, keepdims=True)) + lam_init)
    tq = q_ref.shape[1]
    sq = Q_STREAM
    nt_dims = (((1,), (1,)), ((), ()))
    chunks = [(kr, vr, slice(k0, min(k0 + KEY_CHUNK, kr.shape[1])))
              for kr, vr in ((k_ref, vt_ref), (kc_ref, vct_ref)) for k0 in range(0, kr.shape[1], KEY_CHUNK)]
    lane = lax.broadcasted_iota(jnp.int32, (sq, HEAD_W), 1)

    qq = []
    for j in range(tq // sq):
        qh = q_ref[0, j * sq:(j + 1) * sq, :]
        zero = jnp.zeros_like(qh)
        qq.append(jnp.concatenate([jnp.where(lane < COMP, qh, zero),
                                   jnp.where(lane >= COMP, qh, zero)], axis=0))

    for j in range(tq // sq):
        prev = acc_scr[j]
        r = 1.0 / prev[HEAD_W:HEAD_W + 1]
        o_t = prev[:HEAD_W, :sq] * r[:, :sq] - prev[:HEAD_W, sq:] * (lam * r[:, sq:])
        inv = lax.rsqrt(jnp.mean(o_t * o_t, axis=0, keepdims=True) + EPS)
        y = (o_t * inv).T * (g_ref[...] * (1.0 - lam_init))
        o_ref[0, j * sq:(j + 1) * sq, :] = y.astype(BF16)

    def scores(chunk, j):
        kr, _, rows = chunk
        return lax.dot_general(kr[0, rows, :], qq[j], nt_dims, preferred_element_type=F32)

    n_streams = tq // sq
    m = [jnp.full((1, 2 * sq), -1e30, F32)] * n_streams
    acc = [jnp.zeros((HEAD_W + ONES_ROWS, 2 * sq), F32)] * n_streams
    s_cur = [scores(chunks[0], j) for j in range(n_streams)]
    for c, (_, vr, rows) in enumerate(chunks):
        ones = jnp.ones((ONES_ROWS, rows.stop - rows.start), BF16)
        vt = jnp.concatenate([vr[0, :, rows], ones], axis=0)
        for j in range(n_streams):
            s = s_cur[j]
            m_new = jnp.maximum(m[j], jnp.max(s, axis=0, keepdims=True))
            e = jnp.exp2((s - m_new).astype(BF16))
            acc[j] = acc[j] * jnp.exp2(m[j] - m_new) + _dot(vt, e)
            m[j] = m_new
            if c + 1 < len(chunks):
                s_cur[j] = scores(chunks[c + 1], j)
    for j in range(n_streams):
        acc_scr[j] = acc[j]


def _attn(q, k, vt, kc, vct, lam_vec, subln_g, lam_init):
    nb, nt, _ = q.shape
    nc = kc.shape[1]
    tq = ATTN_Q_TILE
    assert nt % KEY_CHUNK == 0 and nt % tq == 0
    n_t = nt // tq
    n_tiles = nb * HEADS * n_t

    def tile(i):
        return i // (HEADS * n_t), (i // n_t) % HEADS, i % n_t

    def cur(i):
        return tile(jnp.minimum(i, n_tiles - 1))

    def prev(i):
        return tile(jnp.maximum(i - 1, 0))

    keys = lambda n: pl.BlockSpec((1, n, HEAD_W), lambda i: (cur(i)[0], 0, cur(i)[1]))
    vals = lambda n: pl.BlockSpec((1, HEAD_W, n), lambda i: (cur(i)[0], cur(i)[1], 0))
    return pl.pallas_call(
        functools.partial(_attn_kernel, lam_init=lam_init),
        grid=(n_tiles + 1,),
        in_specs=[pl.BlockSpec((1, tq, HEAD_W), lambda i: (cur(i)[0], cur(i)[2], cur(i)[1])),
                  keys(nt), vals(nt), keys(nc), vals(nc),
                  _const_spec(lam_vec.shape), _const_spec((1, HEAD_W))],
        out_specs=pl.BlockSpec((1, tq, HEAD_W), lambda i: (prev(i)[0], prev(i)[2], prev(i)[1])),
        out_shape=jax.ShapeDtypeStruct((nb, nt, QKV_W), BF16),
        scratch_shapes=[pltpu.VMEM((tq // Q_STREAM, HEAD_W + ONES_ROWS, 2 * Q_STREAM), F32)],
        compiler_params=pltpu.CompilerParams(
            dimension_semantics=("arbitrary",), vmem_limit_bytes=VMEM_LIMIT),
        name="attn",
    )(q, k, vt, kc, vct, lam_vec, subln_g.reshape(1, HEAD_W))


def _fourier_consts():
    r, s, gc = DFT_R, SUB, FOURIER_GROUP_CH
    seq = r ** 3
    idx = np.arange(r)
    ang = 2.0 * np.pi * np.outer(idx, idx) / r
    c16, s16 = np.cos(ang), np.sin(ang)
    eye = np.eye(s)
    interleave = lambda m: m.reshape(2, r, s, -1).transpose(1, 0, 2, 3).reshape(2 * r * s, -1)
    ma = np.zeros((2, 2, s, s, r, s))
    for ka3 in range(2):
        for kal in range(s):
            for cl in range(s):
                ma[0, ka3, cl, kal, :, cl] = c16[8 * ka3 + kal]
                ma[1, ka3, cl, kal, :, cl] = -s16[8 * ka3 + kal]
    ma = interleave(ma.reshape(2 * r * s, r * s))
    kc_, ks_ = np.kron(c16, eye), np.kron(s16, eye)
    mb = interleave(np.block([[kc_, ks_], [-ks_, kc_]]))
    cc = np.zeros((r, s, s, 2, s))
    sc = np.zeros((r, s, s, 2, s))
    for cl in range(s):
        for c3 in range(2):
            for kal in range(s):
                cc[:, kal, cl, c3, kal] = c16[:, 8 * c3 + cl]
                sc[:, kal, cl, c3, kal] = s16[:, 8 * c3 + cl]
    cc, sc = cc.reshape(r * s, r * s), sc.reshape(r * s, r * s)
    mc = interleave(np.block([[cc, sc], [-sc, cc]]))
    def cdiag(theta):
        c, sn = np.diag(np.cos(theta)), np.diag(np.sin(theta))
        return np.block([[c, sn], [-sn, c]])

    mbg = np.zeros((2, s, 2, 2 * r * s, 2 * r * s))
    b_, kal_ = np.meshgrid(idx, np.arange(s), indexing="ij")
    for ka3 in range(2):
        for cl in range(s):
            for c3 in range(2):
                theta = 2.0 * np.pi * (16 * b_ + 8 * c3 + cl) * (8 * ka3 + kal_) / seq
                mbg[ka3, cl, c3] = mb @ cdiag(theta.reshape(-1))
    mbg = mbg.reshape(2 * r, 2 * r * s, 2 * r * s)
    mcg = np.zeros((r, 2 * r * s, 2 * r * s))
    cl_, c3_, kal_ = np.meshgrid(np.arange(s), np.arange(2), np.arange(s), indexing="ij")
    for kb in range(r):
        theta = 2.0 * np.pi * (8 * c3_ + cl_) * kb / (r * r) + 0.0 * kal_
        mcg[kb] = mc @ cdiag(theta.reshape(-1))
    ach = 2.0 * np.pi * np.outer(np.arange(gc), np.arange(gc)) / gc
    cd = np.concatenate([np.cos(ach), np.sin(ach)], axis=0) / math.sqrt(seq * gc)
    bf = lambda t: jnp.asarray(t, dtype=F32).astype(BF16)
    return bf(ma), bf(mbg), bf(mcg), bf(cd)


def _fourier_kernel(h_ref, ma_ref, mb_ref, mc_ref, cd_ref, o_ref, sa_scr, sb_scr):
    r, s, gc = DFT_R, SUB, FOURIER_GROUP_CH
    rows = r * s

    def split(y):
        y4 = y.reshape(r, 2, s, gc)
        return y4[:, 0].reshape(rows, gc), y4[:, 1].reshape(rows, gc)

    for b in range(r):
        for c3 in range(2):
            x = h_ref[0, :, 2 * b + c3, :, :].reshape(rows, gc).astype(BF16)
            re, im = split(_dot(ma_ref[...], x))
            sa_scr[0, :, :, b, c3] = re.reshape(2, s, s, gc)
            sa_scr[1, :, :, b, c3] = im.reshape(2, s, s, gc)
    for ka3 in range(2):
        for cl in range(s):
            for c3 in range(2):
                g = (ka3 * s + cl) * 2 + c3
                x = jnp.concatenate([sa_scr[0, ka3, cl, :, c3].reshape(rows, gc),
                                     sa_scr[1, ka3, cl, :, c3].reshape(rows, gc)], axis=0).astype(BF16)
                re, im = split(_dot(mb_ref[g], x))
                sb_scr[0, ka3, cl, :, c3] = re.reshape(r, s, gc)
                sb_scr[1, ka3, cl, :, c3] = im.reshape(r, s, gc)
    for ka3 in range(2):
        for kb in range(r):
            x = jnp.concatenate([sb_scr[0, ka3, :, kb, :].reshape(rows, gc),
                                 sb_scr[1, ka3, :, kb, :].reshape(rows, gc)], axis=0).astype(BF16)
            f = _dot(mc_ref[kb], x)
            fri = jnp.concatenate(split(f), axis=1).astype(BF16)
            o_ref[0, :, kb, ka3] = _dot(fri, cd_ref[...]).reshape(r, s, gc)


def _fourier(h, consts):
    nb, seq, _ = h.shape
    r, s, gc = DFT_R, SUB, FOURIER_GROUP_CH
    h5 = h.reshape(nb, r, 2 * r, s, D)
    stage_scratch = pltpu.VMEM((2, 2, s, r, 2, s, gc), F32)
    out = pl.pallas_call(
        _fourier_kernel,
        grid=(nb, D // gc),
        in_specs=[pl.BlockSpec((1, r, 2 * r, s, gc), lambda b, c: (b, 0, 0, 0, c))]
        + [_const_spec(a.shape) for a in consts],
        out_specs=pl.BlockSpec((1, r, r, 2, s, gc), lambda b, c: (b, 0, 0, 0, 0, c)),
        out_shape=jax.ShapeDtypeStruct((nb, r, r, 2, s, D), F32),
        scratch_shapes=[stage_scratch, stage_scratch],
        compiler_params=pltpu.CompilerParams(
            dimension_semantics=("parallel", "parallel"), vmem_limit_bytes=VMEM_LIMIT),
        name="fourier",
    )(h5, *consts)
    return out.reshape(nb, seq, D)


def kernel(x, c, ctx, c_ctx, ada_w, ada_b, ffn_w_gu, ffn_w_down, mix_w_in, mix_w_out, diff_lambda,
           diff_subln_g, sgu_norm_g, sgu_norm_b, sgu_w, sgu_b, fourier_w_out, fourier_b_out, final_norm_g):
    nb, seq, _ = x.shape
    assert seq == DFT_R ** 3 and DEPTH == 2
    ctx_row = nb
    cc = jnp.concatenate([c, c_ctx[None, :], jnp.zeros((2 * SUB - nb - 1, D), F32)], axis=0)
    mods = _ada(cc, ada_w, ada_b).reshape(DEPTH, cc.shape[0], N_MOD, D)
    latent = lambda b: b
    shared = lambda b: ctx_row

    ffn_w = (ffn_w_gu, ffn_w_down)

    x, h = _ffn(x, mods[0], latent, *ffn_w, (0, 0), mod_idx=(0, 1, 2), emit_idx=(3, 4))
    _, hc = _ffn(ctx.reshape(1, -1, D), mods[0], shared, *ffn_w, (0, 0), mod_idx=(0, 1, 2), emit_idx=(3, 4))
    hc = hc.reshape(ctx.shape)

    w_in = mix_w_in[0].astype(BF16)
    cos_t, sa_t, sb_t = _rope_tables(seq)
    bs_rows = jnp.repeat(sgu_b[0].T, GMLP_CHUNK, axis=1)
    q, k, vt, gated = _inproj(h, w_in, cos_t, sa_t, sb_t,
                             sgu_norm_g[0].reshape(1, GMLP_W), sgu_norm_b[0].reshape(1, GMLP_W),
                             sgu_w[0].astype(BF16), bs_rows)
    kc, vct = _ctx_kv(hc, w_in)
    lam_init = 0.8 - 0.6 * math.exp(-0.3 * 0)
    att = _attn(q, k, vt, kc, vct,
                diff_lambda[0], diff_subln_g[0], lam_init)
    w_out = mix_w_out[0].astype(BF16)
    x = _ffn(x, mods[0], latent, *ffn_w, (0, 1), mod_idx=(6, 7, 8),
             mixes=((att, w_out[:QKV_W]), (gated, w_out[QKV_W:])))

    x, h = _ffn(x, mods[1], latent, *ffn_w, (1, 0), mod_idx=(0, 1, 2), emit_idx=(3, 4), emit_dtype=F32)
    f = _fourier(h, _fourier_consts())
    return _ffn(x, mods[1], latent, *ffn_w, (1, 1), mod_idx=(6, 7, 8),
                mixes=((f, fourier_w_out[0].astype(BF16)),), mix_bias=fourier_b_out[0],
                final_g=final_norm_g)
```
